```python
import math
import jax, jax.numpy as jnp
from jax import lax
import numpy as np

D_MODEL = 1024
BATCH = 16
SEQ = 4096
DEPTH = 1

SSM_WIDTH = D_MODEL // 2
SSM_GROUP = 16
SSM_GROUPS = SSM_WIDTH // SSM_GROUP
SSM_STATE = 64
DT_MIN = 0.001
DT_MAX = 0.1

HEAD_DIM = 64
HEADS_PER_GROUP = 4
ATTN_PATTERNS = ((128, 1), (512, 4), (2048, 16))
N_HEADS = HEADS_PER_GROUP * len(ATTN_PATTERNS)
ATTN_WIDTH = N_HEADS * HEAD_DIM
ATTN_OUT_WIDTH = HEADS_PER_GROUP * HEAD_DIM
ROPE_THETA = 10000.0

N_BRANCHES = 2
IN_WIDTH = SSM_WIDTH + 3 * ATTN_WIDTH + N_BRANCHES * D_MODEL

D_FF = -(-8 * D_MODEL // (3 * 256)) * 256
CONV_WIDTH = 3

LN_EPS = 1e-5
NEG_INF = -1e30
DEEPNORM_ALPHA = (2 * DEPTH) ** 0.25
DEEPNORM_BETA = (8 * DEPTH) ** -0.25

kernel_name = "hybrid_s5_dilated_attn_convffn_deepnorm"


def layer_norm(x, g, b):
    xf = x.astype(jnp.float32)
    mu = jnp.mean(xf, axis=-1, keepdims=True)
    var = jnp.mean(jnp.square(xf - mu), axis=-1, keepdims=True)
    y = (xf - mu) * lax.rsqrt(var + LN_EPS)
    return (y * g.astype(jnp.float32) + b.astype(jnp.float32)).astype(x.dtype)


def rotary(t, positions):
    half = HEAD_DIM // 2
    inv_freq = jnp.power(ROPE_THETA, -jnp.arange(half, dtype=jnp.float32) * 2.0 / HEAD_DIM)
    ang = positions.astype(jnp.float32)[..., None] * inv_freq
    cos = jnp.cos(ang)[:, :, None, :]
    sin = jnp.sin(ang)[:, :, None, :]
    tf = t.astype(jnp.float32)
    t1, t2 = tf[..., :half], tf[..., half:]
    return jnp.concatenate([t1 * cos - t2 * sin, t2 * cos + t1 * sin], axis=-1).astype(t.dtype)


def _cmul(ar, ai, br, bi):
    return ar * br - ai * bi, ar * bi + ai * br


def _linear_recurrence(e1, e2):
    a1r, a1i, b1r, b1i = e1
    a2r, a2i, b2r, b2i = e2
    ar, ai = _cmul(a2r, a2i, a1r, a1i)
    br, bi = _cmul(a2r, a2i, b1r, b1i)
    return (ar, ai, br + b2r, bi + b2i)


def bidirectional_s5(u, lam_re, lam_im, log_dt, b_re, b_im, c_re, c_im, d_skip):
    bsz, s, _ = u.shape
    uf = u.astype(jnp.float32)
    ug = uf.reshape(bsz, s, SSM_GROUPS, SSM_GROUP)
    y = uf * d_skip.astype(jnp.float32)
    for direction in range(2):
        lr = lam_re[direction].astype(jnp.float32)
        li = lam_im[direction].astype(jnp.float32)
        dt = jnp.exp(log_dt[direction].astype(jnp.float32))[:, None]
        xr, xi = lr * dt, li * dt
        abar_m1_r = jnp.expm1(xr) * jnp.cos(xi) - 2.0 * jnp.square(jnp.sin(0.5 * xi))
        abar_i = jnp.exp(xr) * jnp.sin(xi)
        abar_r = abar_m1_r + 1.0
        den = lr * lr + li * li
        kr = (abar_m1_r * lr + abar_i * li) / den
        ki = (abar_i * lr - abar_m1_r * li) / den
        bbr, bbi = _cmul(kr[..., None], ki[..., None],
                         b_re[direction].astype(jnp.float32), b_im[direction].astype(jnp.float32))
        bu_r = jnp.einsum('bsgc,gpc->bsgp', ug, bbr)
        bu_i = jnp.einsum('bsgc,gpc->bsgp', ug, bbi)
        a_r = jnp.broadcast_to(abar_r, (1, s) + abar_r.shape)
        a_i = jnp.broadcast_to(abar_i, (1, s) + abar_i.shape)
        _, _, s_r, s_i = lax.associative_scan(
            _linear_recurrence, (a_r, a_i, bu_r, bu_i), reverse=(direction == 1), axis=1)
        y_dir = (jnp.einsum('bsgp,gcp->bsgc', s_r, c_re[direction].astype(jnp.float32))
                 - jnp.einsum('bsgp,gcp->bsgc', s_i, c_im[direction].astype(jnp.float32)))
        y = y + y_dir.reshape(bsz, s, SSM_WIDTH)
    return y


def dilated_window_attention(q, k, v, dilation, half):
    bsz, s, h, dh = q.shape
    n_sub = s // dilation
    blk = half
    nb = -(-n_sub // blk)
    lp = nb * blk

    def to_blocks(t, extra):
        t = t.reshape(bsz, n_sub, dilation, h, dh).transpose(0, 2, 3, 1, 4)
        t = jnp.pad(t, ((0, 0), (0, 0), (0, 0), (extra, lp - n_sub + extra), (0, 0)))
        return t.reshape(bsz, dilation, h, nb + 2 * (extra // blk), blk, dh)

    def neighbourhood(t):
        tb = to_blocks(t, blk)
        return jnp.concatenate([tb[:, :, :, :-2], tb[:, :, :, 1:-1], tb[:, :, :, 2:]], axis=4)

    qb = to_blocks(q, 0)
    kb = neighbourhood(k)
    vb = neighbourhood(v)
    qi = jnp.arange(blk)[:, None]
    kj = jnp.arange(3 * blk)[None, :]
    step = kj - blk - qi
    key_pos = jnp.arange(nb)[:, None, None] * blk + kj[None] - blk
    mask = (jnp.abs(step) <= half)[None] & (key_pos >= 0) & (key_pos < n_sub)
    scores = jnp.einsum('brhnqd,brhnkd->brhnqk', qb, kb).astype(jnp.float32) * (dh ** -0.5)
    scores = jnp.where(mask, scores, NEG_INF)
    m = jnp.max(scores, axis=-1, keepdims=True)
    p = jnp.exp(scores - m)
    den = jnp.sum(p, axis=-1)
    out = jnp.einsum('brhnqk,brhnkd->brhnqd', p, vb.astype(jnp.float32)) / den[..., None]
    lse = m[..., 0] + jnp.log(den)
    out = out.reshape(bsz, dilation, h, lp, dh)[:, :, :, :n_sub]
    out = out.transpose(0, 3, 1, 2, 4).reshape(bsz, s, h, dh)
    lse = lse.reshape(bsz, dilation, h, lp)[..., :n_sub].transpose(0, 3, 1, 2).reshape(bsz, s, h)
    return out, lse


def depthwise_conv(t, w, b):
    c = t.shape[-1]
    y = lax.conv_general_dilated(
        t, w[:, None, :].astype(t.dtype), window_strides=(1,),
        padding=((CONV_WIDTH // 2, CONV_WIDTH // 2),),
        dimension_numbers=('NWC', 'WIO', 'NWC'), feature_group_count=c)
    return y + b


def hybrid_layer(x, positions, w_in, b_in, ssm_lam_re, ssm_lam_im, ssm_log_dt, ssm_b_re, ssm_b_im,
                 ssm_c_re, ssm_c_im, ssm_d, w_glu_v, w_glu_g, w_attn_br, w_out, ln1_g, ln1_b,
                 w_up, conv_w, conv_b, w_down, ln2_g, ln2_b):
    bsz, s, _ = x.shape
    proj = x @ w_in + b_in
    u_ssm, q, k, v, gates = jnp.split(
        proj, [SSM_WIDTH, SSM_WIDTH + ATTN_WIDTH, SSM_WIDTH + 2 * ATTN_WIDTH,
               SSM_WIDTH + 3 * ATTN_WIDTH], axis=-1)

    y_ssm = bidirectional_s5(u_ssm, ssm_lam_re, ssm_lam_im, ssm_log_dt, ssm_b_re, ssm_b_im,
                             ssm_c_re, ssm_c_im, ssm_d)
    z = jax.nn.gelu(y_ssm, approximate=False).astype(x.dtype)
    ssm_out = (z @ w_glu_v) * jax.nn.sigmoid(z @ w_glu_g)

    q = rotary(q.reshape(bsz, s, N_HEADS, HEAD_DIM), positions)
    k = rotary(k.reshape(bsz, s, N_HEADS, HEAD_DIM), positions)
    v = v.reshape(bsz, s, N_HEADS, HEAD_DIM)
    outs, lses = [], []
    for gi, (window, dilation) in enumerate(ATTN_PATTERNS):
        hs = slice(gi * HEADS_PER_GROUP, (gi + 1) * HEADS_PER_GROUP)
        o, l = dilated_window_attention(q[:, :, hs], k[:, :, hs], v[:, :, hs],
                                        dilation, window // (2 * dilation))
        outs.append(o)
        lses.append(l)
    wts = jax.nn.softmax(jnp.stack(lses, axis=0), axis=0)
    attn = jnp.sum(wts[..., None] * jnp.stack(outs, axis=0), axis=0)
    attn_out = attn.reshape(bsz, s, ATTN_OUT_WIDTH).astype(x.dtype) @ w_attn_br

    g_ssm, g_attn = jnp.split(jax.nn.sigmoid(gates), N_BRANCHES, axis=-1)
    mixed = (g_ssm * ssm_out + g_attn * attn_out) @ w_out
    h = layer_norm(DEEPNORM_ALPHA * x + mixed, ln1_g, ln1_b)

    up = depthwise_conv(h @ w_up, conv_w, conv_b)
    a, val = jnp.split(up, 2, axis=-1)
    ffn = (jax.nn.gelu(a, approximate=False) * val) @ w_down
    return layer_norm(DEEPNORM_ALPHA * h + ffn, ln2_g, ln2_b)


def setup_inputs(seed: int = 0) -> dict:
    key = jax.random.key(seed)
    ks = jax.random.split(key, 26)
    f32 = jnp.float32
    L, G, P, HG = DEPTH, SSM_GROUPS, SSM_STATE, SSM_GROUP

    def normal(k, shape, scale):
        return jax.random.normal(k, shape, f32) * scale

    x = normal(ks[0], (BATCH, SEQ, D_MODEL), 1.0)
    positions = jnp.tile(jnp.arange(SEQ, dtype=jnp.int32)[None, :], (BATCH, 1))
    w_in = normal(ks[1], (L, D_MODEL, IN_WIDTH), D_MODEL ** -0.5)
    b_in = normal(ks[2], (L, IN_WIDTH), 0.02)
    ssm_lam_re = -0.5 + normal(ks[3], (L, 2, G, P), 0.01)
    ssm_lam_im = math.pi * jnp.arange(P, dtype=f32) + normal(ks[4], (L, 2, G, P), 0.01)
    ssm_log_dt = jax.random.uniform(ks[5], (L, 2, G), f32, math.log(DT_MIN), math.log(DT_MAX))
    ssm_b_re = normal(ks[6], (L, 2, G, P, HG), (2 * HG) ** -0.5)
    ssm_b_im = normal(ks[7], (L, 2, G, P, HG), (2 * HG) ** -0.5)
    ssm_c_re = normal(ks[8], (L, 2, G, HG, P), (2 * P) ** -0.5)
    ssm_c_im = normal(ks[9], (L, 2, G, HG, P), (2 * P) ** -0.5)
    ssm_d = normal(ks[10], (L, SSM_WIDTH), 0.5)
    w_glu_v = normal(ks[11], (L, SSM_WIDTH, D_MODEL), SSM_WIDTH ** -0.5)
    w_glu_g = normal(ks[12], (L, SSM_WIDTH, D_MODEL), SSM_WIDTH ** -0.5)
    w_attn_br = normal(ks[13], (L, ATTN_OUT_WIDTH, D_MODEL), ATTN_OUT_WIDTH ** -0.5)
    w_out = normal(ks[14], (L, D_MODEL, D_MODEL), D_MODEL ** -0.5 * DEEPNORM_BETA)
    ln1_g = 1.0 + normal(ks[15], (L, D_MODEL), 0.02)
    ln1_b = normal(ks[16], (L, D_MODEL), 0.02)
    w_up = normal(ks[17], (L, D_MODEL, 2 * D_FF), D_MODEL ** -0.5)
    conv_w = normal(ks[18], (L, CONV_WIDTH, 2 * D_FF), CONV_WIDTH ** -0.5)
    conv_b = normal(ks[19], (L, 2 * D_FF), 0.02)
    w_down = normal(ks[20], (L, D_FF, D_MODEL), D_FF ** -0.5 * DEEPNORM_BETA)
    ln2_g = 1.0 + normal(ks[21], (L, D_MODEL), 0.02)
    ln2_b = normal(ks[22], (L, D_MODEL), 0.02)
    return {"x": x, "positions": positions, "w_in": w_in, "b_in": b_in,
            "ssm_lam_re": ssm_lam_re, "ssm_lam_im": ssm_lam_im, "ssm_log_dt": ssm_log_dt,
            "ssm_b_re": ssm_b_re, "ssm_b_im": ssm_b_im, "ssm_c_re": ssm_c_re, "ssm_c_im": ssm_c_im,
            "ssm_d": ssm_d, "w_glu_v": w_glu_v, "w_glu_g": w_glu_g, "w_attn_br": w_attn_br,
            "w_out": w_out, "ln1_g": ln1_g, "ln1_b": ln1_b, "w_up": w_up, "conv_w": conv_w,
            "conv_b": conv_b, "w_down": w_down, "ln2_g": ln2_g, "ln2_b": ln2_b}


def reference(x, positions, w_in, b_in, ssm_lam_re, ssm_lam_im, ssm_log_dt, ssm_b_re, ssm_b_im,
              ssm_c_re, ssm_c_im, ssm_d, w_glu_v, w_glu_g, w_attn_br, w_out, ln1_g, ln1_b,
              w_up, conv_w, conv_b, w_down, ln2_g, ln2_b):
    h = x
    for layer in range(DEPTH):
        h = hybrid_layer(h, positions, w_in[layer], b_in[layer], ssm_lam_re[layer], ssm_lam_im[layer],
                         ssm_log_dt[layer], ssm_b_re[layer], ssm_b_im[layer], ssm_c_re[layer],
                         ssm_c_im[layer], ssm_d[layer], w_glu_v[layer], w_glu_g[layer],
                         w_attn_br[layer], w_out[layer], ln1_g[layer], ln1_b[layer], w_up[layer],
                         conv_w[layer], conv_b[layer], w_down[layer], ln2_g[layer], ln2_b[layer])
    return h
```

```python
import functools
import math

import numpy as np
import jax
import jax.numpy as jnp
from jax import lax
from jax.experimental import pallas as pl
from jax.experimental.pallas import tpu as pltpu

D_MODEL = 1024
SSM_WIDTH = 512
SSM_GROUP = 16
SSM_GROUPS = 32
SSM_STATE = 64
HEAD_DIM = 64
HEADS_PER_GROUP = 4
ATTN_PATTERNS = ((128, 1), (512, 4), (2048, 16))
N_HEADS = 12
ATTN_WIDTH = 768
ATTN_OUT_WIDTH = 256
D_FF = 2816
LN_EPS = 1e-5
NEG_INF = -1e30
ROPE_THETA = 10000.0
DEEPNORM_ALPHA = 2.0 ** 0.25

CHUNK = 16
PAIR_W = 2 * SSM_GROUP * CHUNK
N_PAIRS = SSM_GROUPS // 2
HALF_WIN = 64
QBLK = 128
KBLK = 256
PROJ_W = SSM_WIDTH + 3 * ATTN_WIDTH
VMEM_LIMIT = 56 * 1024 * 1024
F32 = jnp.float32
BF16 = jnp.bfloat16


def _dot(a, b):
    return jnp.dot(a, b, preferred_element_type=F32)


def _gelu(x):
    return 0.5 * x * (1.0 + lax.erf(x * np.float32(math.sqrt(0.5))))


def _layer_norm(r, g, b):
    mu = jnp.mean(r, axis=-1, keepdims=True)
    c = r - mu
    var = jnp.mean(c * c, axis=-1, keepdims=True)
    return c * lax.rsqrt(var + LN_EPS) * g + b


def _proj_kernel(x_ref, pos_ref, w_ref, b_ref, invf_ref, sgn_ref, u_ref, q_ref, k_ref, v_ref):
    xb = x_ref[...].astype(BF16)
    ang = pos_ref[...].astype(F32) * invf_ref[...]
    cos = jnp.cos(ang)
    sin = jnp.sin(ang) * sgn_ref[...]
    u = _dot(xb, w_ref[:, 0:SSM_WIDTH]) + b_ref[:, 0:SSM_WIDTH]
    u_ref[...] = u.astype(BF16)
    for dst, c0 in ((q_ref, SSM_WIDTH), (k_ref, SSM_WIDTH + ATTN_WIDTH)):
        t = _dot(xb, w_ref[:, c0:c0 + ATTN_WIDTH]) + b_ref[:, c0:c0 + ATTN_WIDTH]
        for j in range(ATTN_WIDTH // 128):
            tj = t[:, 128 * j:128 * (j + 1)]
            dst[:, 128 * j:128 * (j + 1)] = (tj * cos + pltpu.roll(tj, 64, 1) * sin).astype(BF16)
    c0 = SSM_WIDTH + 2 * ATTN_WIDTH
    v = _dot(xb, w_ref[:, c0:c0 + ATTN_WIDTH]) + b_ref[:, c0:c0 + ATTN_WIDTH]
    v_ref[...] = v.astype(BF16)


def _proj(x2, pos, w, b, invf, sgn, tm):
    t = x2.shape[0]
    row = lambda i: (i, 0)
    fixed = lambda i: (0, 0)
    return pl.pallas_call(
        _proj_kernel,
        grid=(t // tm,),
        in_specs=[pl.BlockSpec((tm, D_MODEL), row),
                  pl.BlockSpec((tm, 1), row),
                  pl.BlockSpec((D_MODEL, PROJ_W), fixed, pipeline_mode=pl.Buffered(1)),
                  pl.BlockSpec((1, PROJ_W), fixed),
                  pl.BlockSpec((1, 128), fixed),
                  pl.BlockSpec((1, 128), fixed)],
        out_specs=[pl.BlockSpec((tm, SSM_WIDTH), row),
                   pl.BlockSpec((tm, ATTN_WIDTH), row),
                   pl.BlockSpec((tm, ATTN_WIDTH), row),
                   pl.BlockSpec((tm, ATTN_WIDTH), row)],
        out_shape=[jax.ShapeDtypeStruct((t, SSM_WIDTH), BF16),
                   jax.ShapeDtypeStruct((t, ATTN_WIDTH), BF16),
                   jax.ShapeDtypeStruct((t, ATTN_WIDTH), BF16),
                   jax.ShapeDtypeStruct((t, ATTN_WIDTH), BF16)],
        compiler_params=pltpu.CompilerParams(dimension_semantics=("arbitrary",), vmem_limit_bytes=VMEM_LIMIT),
        name="proj",
    )(x2, pos, w, b, invf, sgn)


def _ssm_kernel(v_ref, m_ref, q_ref, p_ref, a_ref, z_ref, h_ref, *, n_chunks, bsz):
    v = v_ref[...]
    h_ref[...] = _dot(v, q_ref[...])
    dec = a_ref[...]
    afr, afi, abr, abi = (jnp.broadcast_to(dec[i:i + 1, :], (bsz, 128)) for i in range(4))

    def step(k, carry):
        hfr, hfi, hbr, hbi = carry
        rf = pl.ds(pl.multiple_of(k * bsz, bsz), bsz)
        rb = pl.ds(pl.multiple_of((n_chunks - 1 - k) * bsz, bsz), bsz)
        xfr = h_ref[rf, 0:128]
        xfi = h_ref[rf, 128:256]
        xbr = h_ref[rb, 256:384]
        xbi = h_ref[rb, 384:512]
        h_ref[rf, 0:128] = hfr
        h_ref[rf, 128:256] = hfi
        h_ref[rb, 256:384] = hbr
        h_ref[rb, 384:512] = hbi
        return (afr * hfr - afi * hfi + xfr, afr * hfi + afi * hfr + xfi,
                abr * hbr - abi * hbi + xbr, abr * hbi + abi * hbr + xbi)

    zero = jnp.zeros((bsz, 128), F32)
    lax.fori_loop(0, n_chunks, step, (zero, zero, zero, zero), unroll=4)
    y = _dot(v, m_ref[...]) + _dot(h_ref[...].astype(BF16), p_ref[...])
    z_ref[...] = _gelu(y).astype(BF16)


def _ssm(vp, m, q, p, a, n_chunks, bsz):
    rows = n_chunks * bsz
    blk = lambda shape: pl.BlockSpec((None,) + shape, lambda i: (i, 0, 0))
    return pl.pallas_call(
        functools.partial(_ssm_kernel, n_chunks=n_chunks, bsz=bsz),
        grid=(N_PAIRS,),
        in_specs=[blk((rows, PAIR_W)), blk((PAIR_W, PAIR_W)), blk((PAIR_W, PAIR_W)),
                  blk((PAIR_W, PAIR_W)), blk((8, 128))],
        out_specs=blk((rows, PAIR_W)),
        out_shape=jax.ShapeDtypeStruct((N_PAIRS, rows, PAIR_W), BF16),
        scratch_shapes=[pltpu.VMEM((rows, PAIR_W), F32)],
        compiler_params=pltpu.CompilerParams(dimension_semantics=("arbitrary",), vmem_limit_bytes=VMEM_LIMIT),
        name="ssm",
    )(vp, m, q, p, a)


def _ssm_params(lam_re, lam_im, log_dt, b_re, b_im, c_re, c_im, d_skip):
    hp = lax.Precision.HIGHEST
    dt = jnp.exp(log_dt)[..., None]
    xr, xi = lam_re * dt, lam_im * dt
    abar_m1_r = jnp.expm1(xr) * jnp.cos(xi) - 2.0 * jnp.square(jnp.sin(0.5 * xi))
    abar_i = jnp.exp(xr) * jnp.sin(xi)
    den = lam_re * lam_re + lam_im * lam_im
    kr = (abar_m1_r * lam_re + abar_i * lam_im) / den
    ki = (abar_i * lam_re - abar_m1_r * lam_im) / den
    bbr = kr[..., None] * b_re - ki[..., None] * b_im
    bbi = kr[..., None] * b_im + ki[..., None] * b_re
    n = jnp.arange(CHUNK + 1, dtype=F32)[:, None, None, None]
    mag = jnp.exp(n * xr[None])
    pr, pi = mag * jnp.cos(n * xi[None]), mag * jnp.sin(n * xi[None])
    wr = c_re[None] * pr[:, :, :, None, :] - c_im[None] * pi[:, :, :, None, :]
    wi = c_re[None] * pi[:, :, :, None, :] + c_im[None] * pr[:, :, :, None, :]
    kern = (jnp.einsum('ndgop,dgpc->ndgoc', wr, bbr, precision=hp)
            - jnp.einsum('ndgop,dgpc->ndgoc', wi, bbi, precision=hp))
    s_idx = jnp.arange(CHUNK)[:, None]
    t_idx = jnp.arange(CHUNK)[None, :]
    lag_f = jnp.clip(t_idx - s_idx, 0, CHUNK)
    lag_b = jnp.clip(s_idx - t_idx, 0, CHUNK)
    kf = jnp.where((s_idx <= t_idx)[:, :, None, None, None], kern[lag_f, 0], 0.0)
    kb = jnp.where((s_idx >= t_idx)[:, :, None, None, None], kern[lag_b, 1], 0.0)
    eye_t = (s_idx == t_idx).astype(F32)[:, :, None, None, None]
    eye_c = jnp.eye(SSM_GROUP, dtype=F32)[None, None, None]
    skip = eye_t * eye_c * d_skip.reshape(SSM_GROUPS, SSM_GROUP)[None, None, :, :, None]
    toep = jnp.transpose(kf + kb + skip, (2, 0, 4, 1, 3))
    toep = toep.reshape(SSM_GROUPS, CHUNK * SSM_GROUP, CHUNK * SSM_GROUP)
    pw_f = jnp.arange(CHUNK - 1, -1, -1)
    pw_b = jnp.arange(CHUNK)

    def in_map(d, pw):
        ar, ai = pr[pw, d], pi[pw, d]
        qr = ar[..., None] * bbr[d][None] - ai[..., None] * bbi[d][None]
        qi = ar[..., None] * bbi[d][None] + ai[..., None] * bbr[d][None]
        f = lambda z: jnp.transpose(z, (1, 0, 3, 2)).reshape(SSM_GROUPS, CHUNK * SSM_GROUP, SSM_STATE)
        return f(qr), f(qi)

    qfr, qfi = in_map(0, pw_f)
    qbr, qbi = in_map(1, pw_b)
    pw_of = jnp.arange(1, CHUNK + 1)
    pw_ob = jnp.arange(CHUNK, 0, -1)

    def out_map(d, pw):
        g = lambda z: jnp.transpose(z[pw, d], (1, 3, 0, 2)).reshape(SSM_GROUPS, SSM_STATE, CHUNK * SSM_GROUP)
        return g(wr), -g(wi)

    pfr, pfi = out_map(0, pw_of)
    pbr, pbi = out_map(1, pw_ob)

    def pair_cols(parts):
        out = jnp.zeros((N_PAIRS, PAIR_W, 4, 2, SSM_STATE), F32)
        for pi_, z in enumerate(parts):
            zz = z.reshape(N_PAIRS, 2, CHUNK * SSM_GROUP, SSM_STATE)
            for gl in range(2):
                out = out.at[:, gl * 256:(gl + 1) * 256, pi_, gl, :].set(zz[:, gl])
        return out.reshape(N_PAIRS, PAIR_W, PAIR_W)

    def pair_rows(parts):
        out = jnp.zeros((N_PAIRS, 4, 2, SSM_STATE, PAIR_W), F32)
        for pi_, z in enumerate(parts):
            zz = z.reshape(N_PAIRS, 2, SSM_STATE, CHUNK * SSM_GROUP)
            for gl in range(2):
                out = out.at[:, pi_, gl, :, gl * 256:(gl + 1) * 256].set(zz[:, gl])
        return out.reshape(N_PAIRS, PAIR_W, PAIR_W)

    q_pair = pair_cols([qfr, qfi, qbr, qbi])
    p_pair = pair_rows([pfr, pfi, pbr, pbi])
    tp = toep.reshape(N_PAIRS, 2, 256, 256)
    m_pair = jnp.zeros((N_PAIRS, PAIR_W, PAIR_W), F32)
    m_pair = m_pair.at[:, 0:256, 0:256].set(tp[:, 0]).at[:, 256:512, 256:512].set(tp[:, 1])
    dec = jnp.stack([pr[CHUNK, 0], pi[CHUNK, 0], pr[CHUNK, 1], pi[CHUNK, 1]], axis=0)
    dec = jnp.transpose(dec.reshape(4, N_PAIRS, 128), (1, 0, 2))
    dec = jnp.concatenate([dec, jnp.zeros((N_PAIRS, 4, 128), F32)], axis=1)
    return m_pair.astype(BF16), q_pair.astype(BF16), p_pair.astype(BF16), dec


def _attn_kernel(q_ref, k_ref, v_ref, o_ref, l_ref, *, n_sub):
    lane = lax.broadcasted_iota(jnp.int32, (QBLK, 128), 1)
    head0 = (lane & 32) == 0
    low64 = lane < 64
    qi = lax.broadcasted_iota(jnp.int32, (2 * QBLK, KBLK), 0) & (QBLK - 1)
    kj = lax.broadcasted_iota(jnp.int32, (2 * QBLK, KBLK), 1)
    rel = kj - qi
    zero = jnp.zeros((QBLK, 128), BF16)

    def block(i, carry):
        m0 = pl.multiple_of(i * QBLK, QBLK)
        ks = pl.multiple_of(jnp.clip(m0 - HALF_WIN, 0, n_sub - KBLK), HALF_WIN)
        off = ks - m0
        band = jnp.abs(rel + off) <= HALF_WIN
        for pi_ in range(2):
            cs = slice(128 * pi_, 128 * (pi_ + 1))
            q2 = q_ref[pl.ds(m0, QBLK), cs]
            k2 = k_ref[pl.ds(ks, KBLK), cs]
            v2 = v_ref[pl.ds(ks, KBLK), cs]
            qq = jnp.concatenate([jnp.where(head0, q2, zero), jnp.where(head0, zero, q2)], axis=0)
            s = lax.dot_general(qq, k2, (((1,), (1,)), ((), ())), preferred_element_type=F32)
            s = jnp.where(band, s, NEG_INF)
            mx = jnp.max(s, axis=-1, keepdims=True)
            p = jnp.exp(s - mx)
            den = jnp.sum(p, axis=-1, keepdims=True)
            pv = _dot(p.astype(BF16), v2)
            lse = mx + jnp.log(den)
            o = jnp.where(low64, pv[:QBLK] / den[:QBLK], pv[QBLK:] / den[QBLK:])
            l = jnp.where(low64, lse[:QBLK], lse[QBLK:])
            o_ref[pl.ds(m0, QBLK), cs] = o
            l_ref[pl.ds(m0, QBLK), cs] = l
        return carry

    lax.fori_loop(0, n_sub // QBLK, block, 0)


def _attn(q, k, v):
    ns, n_sub, w = q.shape
    blk = pl.BlockSpec((None, n_sub, w), lambda i: (i, 0, 0))
    return pl.pallas_call(
        functools.partial(_attn_kernel, n_sub=n_sub),
        grid=(ns,),
        in_specs=[blk, blk, blk],
        out_specs=[blk, blk],
        out_shape=[jax.ShapeDtypeStruct((ns, n_sub, w), F32), jax.ShapeDtypeStruct((ns, n_sub, w), F32)],
        compiler_params=pltpu.CompilerParams(dimension_semantics=("arbitrary",), vmem_limit_bytes=VMEM_LIMIT),
        name="attn",
    )(q, k, v)


def _mix_kernel(x_ref, z_ref, o0_ref, o1_ref, o2_ref, l0_ref, l1_ref, l2_ref,
                wg_ref, bg_ref, wglu_ref, wbr_ref, wout_ref, g_ref, b_ref, h_ref):
    x = x_ref[...]
    xb = x.astype(BF16)
    gates = jax.nn.sigmoid(_dot(xb, wg_ref[...]) + bg_ref[...])
    gv = _dot(z_ref[...], wglu_ref[...])
    ssm_out = gv[:, :D_MODEL] * jax.nn.sigmoid(gv[:, D_MODEL:])
    l0, l1, l2 = l0_ref[...], l1_ref[...], l2_ref[...]
    mx = jnp.maximum(jnp.maximum(l0, l1), l2)
    e0, e1, e2 = jnp.exp(l0 - mx), jnp.exp(l1 - mx), jnp.exp(l2 - mx)
    attn = (e0 * o0_ref[...] + e1 * o1_ref[...] + e2 * o2_ref[...]) / (e0 + e1 + e2)
    attn_out = _dot(attn.astype(BF16), wbr_ref[...])
    mix = gates[:, :D_MODEL] * ssm_out + gates[:, D_MODEL:] * attn_out
    mixed = _dot(mix.astype(BF16), wout_ref[...])
    h_ref[...] = _layer_norm(DEEPNORM_ALPHA * x + mixed, g_ref[...], b_ref[...])


def _mix(x2, z, outs, lses, wg, bg, wglu, wbr, wout, g, b, tm):
    t = x2.shape[0]
    row = lambda i: (i, 0)
    fixed = lambda i: (0, 0)
    rowspec = lambda w: pl.BlockSpec((tm, w), row)
    wspec = lambda shape: pl.BlockSpec(shape, fixed, pipeline_mode=pl.Buffered(1))
    return pl.pallas_call(
        _mix_kernel,
        grid=(t // tm,),
        in_specs=[rowspec(D_MODEL), rowspec(SSM_WIDTH)] + [rowspec(ATTN_OUT_WIDTH)] * 6
        + [wspec((D_MODEL, 2 * D_MODEL)), wspec((1, 2 * D_MODEL)), wspec((SSM_WIDTH, 2 * D_MODEL)),
           wspec((ATTN_OUT_WIDTH, D_MODEL)), wspec((D_MODEL, D_MODEL)), wspec((1, D_MODEL)), wspec((1, D_MODEL))],
        out_specs=rowspec(D_MODEL),
        out_shape=jax.ShapeDtypeStruct((t, D_MODEL), F32),
        compiler_params=pltpu.CompilerParams(dimension_semantics=("arbitrary",), vmem_limit_bytes=VMEM_LIMIT),
        name="mix",
    )(x2, z, *outs, *lses, wg, bg, wglu, wbr, wout, g, b)


FFN_CHUNK = 256


def _ffn_kernel(h_ref, hp_ref, hn_ref, wup_ref, cw_ref, cb_ref, wdn_ref, g_ref, b_ref, o_ref, s_ref,
                *, tm, tiles_per_seq):
    i = pl.program_id(0)
    j = i % tiles_per_seq
    h = h_ref[...]
    hb = h.astype(BF16)
    prev_ok = (j > 0).astype(F32)
    next_ok = (j < tiles_per_seq - 1).astype(F32)
    hpb = (hp_ref[...] * prev_ok).astype(BF16)
    hnb = (hn_ref[...] * next_ok).astype(BF16)
    acc = jnp.zeros((tm, D_MODEL), F32)
    for c in range(D_FF // FFN_CHUNK):
        halves = []
        for c0 in (c * FFN_CHUNK, D_FF + c * FFN_CHUNK):
            w = wup_ref[:, c0:c0 + FFN_CHUNK]
            s_ref[8:8 + tm, :] = _dot(hb, w)
            s_ref[0:8, :] = _dot(hpb, w)
            s_ref[8 + tm:16 + tm, :] = _dot(hnb, w)
            cw = cw_ref[:, c0:c0 + FFN_CHUNK]
            halves.append(s_ref[7:7 + tm, :] * cw[0:1] + s_ref[8:8 + tm, :] * cw[1:2]
                          + s_ref[9:9 + tm, :] * cw[2:3] + cb_ref[:, c0:c0 + FFN_CHUNK])
        a, val = halves
        acc = acc + _dot((_gelu(a) * val).astype(BF16), wdn_ref[c * FFN_CHUNK:(c + 1) * FFN_CHUNK, :])
    o_ref[...] = _layer_norm(DEEPNORM_ALPHA * h + acc, g_ref[...], b_ref[...])


def _ffn(h, wup, cw, cb, wdn, g, b, tm, seq):
    t = h.shape[0]
    tps = seq // tm
    nblk8 = t // 8
    fixed = lambda i: (0, 0)
    wspec = lambda shape: pl.BlockSpec(shape, fixed, pipeline_mode=pl.Buffered(1))
    return pl.pallas_call(
        functools.partial(_ffn_kernel, tm=tm, tiles_per_seq=tps),
        grid=(t // tm,),
        in_specs=[pl.BlockSpec((tm, D_MODEL), lambda i: (i, 0)),
                  pl.BlockSpec((8, D_MODEL), lambda i: (jnp.maximum(i * (tm // 8) - 1, 0), 0)),
                  pl.BlockSpec((8, D_MODEL), lambda i: (jnp.minimum((i + 1) * (tm // 8), nblk8 - 1), 0)),
                  wspec((D_MODEL, 2 * D_FF)), wspec((8, 2 * D_FF)), wspec((1, 2 * D_FF)),
                  wspec((D_FF, D_MODEL)), wspec((1, D_MODEL)), wspec((1, D_MODEL))],
        out_specs=pl.BlockSpec((tm, D_MODEL), lambda i: (i, 0)),
        out_shape=jax.ShapeDtypeStruct((t, D_MODEL), F32),
        scratch_shapes=[pltpu.VMEM((tm + 16, FFN_CHUNK), F32)],
        compiler_params=pltpu.CompilerParams(dimension_semantics=("arbitrary",), vmem_limit_bytes=VMEM_LIMIT),
        name="ffn",
    )(h, h, h, wup, cw, cb, wdn, g, b)


def _qk_column_order():
    cols = []
    for pair in range(N_HEADS // 2):
        for part in range(2):
            for hh in range(2):
                head = 2 * pair + hh
                cols.extend(head * HEAD_DIM + part * 32 + j for j in range(32))
    return np.asarray(cols, dtype=np.int32)


def _layer(x, positions, w_in, b_in, lam_re, lam_im, log_dt, b_re, b_im, c_re, c_im, d_skip,
           w_glu_v, w_glu_g, w_attn_br, w_out, ln1_g, ln1_b, w_up, conv_w, conv_b, w_down, ln2_g, ln2_b):
    bsz, seq, _ = x.shape
    t = bsz * seq
    tm = 512
    n_chunks = seq // CHUNK
    x2 = x.reshape(t, D_MODEL)

    perm = _qk_column_order()
    q0, k0, v0, g0 = SSM_WIDTH, SSM_WIDTH + ATTN_WIDTH, SSM_WIDTH + 2 * ATTN_WIDTH, SSM_WIDTH + 3 * ATTN_WIDTH
    scale = HEAD_DIM ** -0.5
    w_proj = jnp.concatenate([w_in[:, :q0], w_in[:, q0:k0][:, perm] * scale, w_in[:, k0:v0][:, perm],
                              w_in[:, v0:g0]], axis=1).astype(BF16)
    b_proj = jnp.concatenate([b_in[:q0], b_in[q0:k0][perm] * scale, b_in[k0:v0][perm], b_in[v0:g0]])[None, :]
    inv_freq = jnp.power(ROPE_THETA, -jnp.arange(32, dtype=F32) * 2.0 / HEAD_DIM)
    invf = jnp.tile(inv_freq, 4)[None, :]
    sgn = jnp.concatenate([-jnp.ones((64,), F32), jnp.ones((64,), F32)])[None, :]

    u, q, k, v = _proj(x2, positions.reshape(t, 1), w_proj, b_proj, invf, sgn, tm)

    m_pair, q_pair, p_pair, dec = _ssm_params(lam_re, lam_im, log_dt, b_re, b_im, c_re, c_im, d_skip)
    vp = u.reshape(bsz, n_chunks, CHUNK, N_PAIRS, 2, SSM_GROUP)
    vp = jnp.transpose(vp, (3, 1, 0, 4, 2, 5)).reshape(N_PAIRS, n_chunks * bsz, PAIR_W)
    z = _ssm(vp, m_pair, q_pair, p_pair, dec, n_chunks, bsz)
    z = z.reshape(N_PAIRS, n_chunks, bsz, 2, CHUNK, SSM_GROUP)
    z = jnp.transpose(z, (2, 1, 4, 0, 3, 5)).reshape(t, SSM_WIDTH)

    outs, lses = [], []
    for gi, (window, dil) in enumerate(ATTN_PATTERNS):
        n_sub = seq // dil
        cs = slice(gi * 256, (gi + 1) * 256)

        def split(a):
            a = a[:, cs].reshape(bsz, n_sub, dil, 256)
            return jnp.transpose(a, (0, 2, 1, 3)).reshape(bsz * dil, n_sub, 256)

        def merge(a):
            a = a.reshape(bsz, dil, n_sub, 256)
            return jnp.transpose(a, (0, 2, 1, 3)).reshape(t, 256)

        o, l = _attn(split(q), split(k), split(v))
        outs.append(merge(o))
        lses.append(merge(l))

    wg = w_in[:, g0:].astype(BF16)
    bg = b_in[g0:][None, :]
    wglu = jnp.concatenate([w_glu_v, w_glu_g], axis=1).astype(BF16)
    h = _mix(x2, z, outs, lses, wg, bg, wglu, w_attn_br.astype(BF16), w_out.astype(BF16),
             ln1_g[None, :], ln1_b[None, :], tm)

    cw = jnp.concatenate([conv_w, jnp.zeros((5, 2 * D_FF), F32)], axis=0)
    out = _ffn(h, w_up.astype(BF16), cw, conv_b[None, :], w_down.astype(BF16), ln2_g[None, :], ln2_b[None, :],
               tm, seq)
    return out.reshape(bsz, seq, D_MODEL)


def kernel(x, positions, w_in, b_in, ssm_lam_re, ssm_lam_im, ssm_log_dt, ssm_b_re, ssm_b_im, ssm_c_re, ssm_c_im, ssm_d, w_glu_v, w_glu_g, w_attn_br, w_out, ln1_g, ln1_b, w_up, conv_w, conv_b, w_down, ln2_g, ln2_b):
    h = x
    for layer in range(w_in.shape[0]):
        h = _layer(h, positions, w_in[layer], b_in[layer], ssm_lam_re[layer], ssm_lam_im[layer],
                   ssm_log_dt[layer], ssm_b_re[layer], ssm_b_im[layer], ssm_c_re[layer], ssm_c_im[layer],
                   ssm_d[layer], w_glu_v[layer], w_glu_g[layer], w_attn_br[layer], w_out[layer],
                   ln1_g[layer], ln1_b[layer], w_up[layer], conv_w[layer], conv_b[layer], w_down[layer],
                   ln2_g[layer], ln2_b[layer])
    return h
```

```python
import functools
import math

import numpy as np
import jax
import jax.numpy as jnp
from jax import lax
from jax.experimental import pallas as pl
from jax.experimental.pallas import tpu as pltpu

D_MODEL = 1024
SSM_WIDTH = 512
SSM_GROUP = 16
SSM_GROUPS = 32
SSM_STATE = 64
HEAD_DIM = 64
HEADS_PER_GROUP = 4
ATTN_PATTERNS = ((128, 1), (512, 4), (2048, 16))
N_HEADS = 12
ATTN_WIDTH = 768
ATTN_OUT_WIDTH = 256
D_FF = 2816
LN_EPS = 1e-5
NEG_INF = -1e30
ROPE_THETA = 10000.0
DEEPNORM_ALPHA = 2.0 ** 0.25

CHUNK = 16
PAIR_W = 2 * SSM_GROUP * CHUNK
N_PAIRS = SSM_GROUPS // 2
HALF_WIN = 64
QBLK = 128
KBLK = 256
PTILE = 256
PROJ_W = SSM_WIDTH + 3 * ATTN_WIDTH
VMEM_LIMIT = 56 * 1024 * 1024
F32 = jnp.float32
BF16 = jnp.bfloat16


def _dot(a, b):
    return jnp.dot(a, b, preferred_element_type=F32)


def _gelu(x):
    return 0.5 * x * (1.0 + lax.erf(x * np.float32(math.sqrt(0.5))))


def _layer_norm(r, g, b):
    mu = jnp.mean(r, axis=-1, keepdims=True)
    c = r - mu
    var = jnp.mean(c * c, axis=-1, keepdims=True)
    return c * lax.rsqrt(var + LN_EPS) * g + b


def _params(n_axes):
    return pltpu.CompilerParams(dimension_semantics=("arbitrary",) * n_axes, vmem_limit_bytes=VMEM_LIMIT)


def _place(r, n_rows):
    lane_grp = lax.broadcasted_iota(jnp.int32, (n_rows, 128), 1) // 32
    out = [[None] * 4 for _ in range(N_PAIRS)]
    for j in range(4):
        for q in range(4):
            src = [r[(4 * j + m) * n_rows:(4 * j + m + 1) * n_rows, 128 * q:128 * (q + 1)] for m in range(4)]
            rolled = [[s if sh == 0 else pltpu.roll(s, 32 * sh, 1) for sh in range(4)] for s in src]
            for pp in range(4):
                d = rolled[0][(0 - pp) % 4]
                for m in range(1, 4):
                    d = jnp.where(lane_grp == m, rolled[m][(m - pp) % 4], d)
                out[4 * q + pp][j] = d
    return out


def _unplace(zs, n_rows):
    lane_grp = lax.broadcasted_iota(jnp.int32, (n_rows, 128), 1) // 32
    rows = []
    for j in range(4):
        for m in range(4):
            cols = []
            for q in range(4):
                d = None
                for pp in range(4):
                    s = zs[4 * q + pp][j]
                    sh = (pp - m) % 4
                    rl = s if sh == 0 else pltpu.roll(s, 32 * sh, 1)
                    d = rl if d is None else jnp.where(lane_grp == pp, rl, d)
                cols.append(d)
            rows.append(jnp.concatenate(cols, axis=1))
    return jnp.concatenate(rows, axis=0)


def _proj_kernel(x_ref, pos_ref, w_ref, b_ref, invf_ref, sgn_ref, perm_ref, vp_ref, q_ref, k_ref, v_ref, *, bsz, ts):
    tm = bsz * ts
    xb = x_ref[...].reshape(tm, D_MODEL).astype(BF16)
    ang = pos_ref[...].reshape(tm, 1).astype(F32) * invf_ref[...]
    cos = jnp.cos(ang)
    sin = jnp.sin(ang) * sgn_ref[...]
    u = _dot(xb, w_ref[:, 0:SSM_WIDTH]) + b_ref[:, 0:SSM_WIDTH]
    r = _dot(perm_ref[...], u.astype(BF16))
    placed = _place(r, tm // CHUNK)
    for p in range(N_PAIRS):
        for j in range(4):
            vp_ref[p, :, 128 * j:128 * (j + 1)] = placed[p][j].astype(BF16)
    for dst, c0 in ((q_ref, SSM_WIDTH), (k_ref, SSM_WIDTH + ATTN_WIDTH)):
        t = _dot(xb, w_ref[:, c0:c0 + ATTN_WIDTH]) + b_ref[:, c0:c0 + ATTN_WIDTH]
        for j in range(ATTN_WIDTH // 128):
            tj = t[:, 128 * j:128 * (j + 1)]
            rot = (tj * cos + pltpu.roll(tj, 64, 1) * sin).astype(BF16)
            dst[:, :, 128 * j:128 * (j + 1)] = rot.reshape(bsz, ts, 128)
    c0 = SSM_WIDTH + 2 * ATTN_WIDTH
    v = _dot(xb, w_ref[:, c0:c0 + ATTN_WIDTH]) + b_ref[:, c0:c0 + ATTN_WIDTH]
    v_ref[...] = v.astype(BF16).reshape(bsz, ts, ATTN_WIDTH)


def _proj(x, pos, w, b, invf, sgn, perm, ts):
    bsz, seq, _ = x.shape
    tm = bsz * ts
    rows = tm // CHUNK
    tile = lambda w_: pl.BlockSpec((bsz, ts, w_), lambda i: (0, i, 0))
    fixed = lambda i: (0, 0)
    return pl.pallas_call(
        functools.partial(_proj_kernel, bsz=bsz, ts=ts),
        grid=(seq // ts,),
        in_specs=[tile(D_MODEL), tile(1),
                  pl.BlockSpec((D_MODEL, PROJ_W), fixed, pipeline_mode=pl.Buffered(1)),
                  pl.BlockSpec((1, PROJ_W), fixed),
                  pl.BlockSpec((1, 128), fixed),
                  pl.BlockSpec((1, 128), fixed),
                  pl.BlockSpec((tm, tm), fixed, pipeline_mode=pl.Buffered(1))],
        out_specs=[pl.BlockSpec((N_PAIRS, rows, PAIR_W), lambda i: (0, i, 0)),
                   tile(ATTN_WIDTH), tile(ATTN_WIDTH), tile(ATTN_WIDTH)],
        out_shape=[jax.ShapeDtypeStruct((N_PAIRS, seq // CHUNK * bsz, PAIR_W), BF16),
                   jax.ShapeDtypeStruct((bsz, seq, ATTN_WIDTH), BF16),
                   jax.ShapeDtypeStruct((bsz, seq, ATTN_WIDTH), BF16),
                   jax.ShapeDtypeStruct((bsz, seq, ATTN_WIDTH), BF16)],
        compiler_params=_params(1),
        name="proj",
    )(x, pos, w, b, invf, sgn, perm)


def _ssm_kernel(v_ref, m_ref, q_ref, p_ref, a_ref, z_ref, h_ref, *, n_chunks, bsz):
    v = v_ref[...]
    h_ref[...] = _dot(v, q_ref[...])
    dec = a_ref[...]
    afr, afi, abr, abi = (jnp.broadcast_to(dec[i:i + 1, :], (bsz, 128)) for i in range(4))

    def step(k, carry):
        hfr, hfi, hbr, hbi = carry
        rf = pl.ds(pl.multiple_of(k * bsz, bsz), bsz)
        rb = pl.ds(pl.multiple_of((n_chunks - 1 - k) * bsz, bsz), bsz)
        xfr = h_ref[rf, 0:128]
        xfi = h_ref[rf, 128:256]
        xbr = h_ref[rb, 256:384]
        xbi = h_ref[rb, 384:512]
        h_ref[rf, 0:128] = hfr
        h_ref[rf, 128:256] = hfi
        h_ref[rb, 256:384] = hbr
        h_ref[rb, 384:512] = hbi
        return (afr * hfr - afi * hfi + xfr, afr * hfi + afi * hfr + xfi,
                abr * hbr - abi * hbi + xbr, abr * hbi + abi * hbr + xbi)

    zero = jnp.zeros((bsz, 128), F32)
    lax.fori_loop(0, n_chunks, step, (zero, zero, zero, zero), unroll=4)
    y = _dot(v, m_ref[...]) + _dot(h_ref[...].astype(BF16), p_ref[...])
    z_ref[...] = _gelu(y).astype(BF16)


def _ssm(vp, m, q, p, a, n_chunks, bsz):
    rows = n_chunks * bsz
    blk = lambda shape: pl.BlockSpec((None,) + shape, lambda i: (i, 0, 0))
    return pl.pallas_call(
        functools.partial(_ssm_kernel, n_chunks=n_chunks, bsz=bsz),
        grid=(N_PAIRS,),
        in_specs=[blk((rows, PAIR_W)), blk((PAIR_W, PAIR_W)), blk((PAIR_W, PAIR_W)),
                  blk((PAIR_W, PAIR_W)), blk((8, 128))],
        out_specs=blk((rows, PAIR_W)),
        out_shape=jax.ShapeDtypeStruct((N_PAIRS, rows, PAIR_W), BF16),
        scratch_shapes=[pltpu.VMEM((rows, PAIR_W), F32)],
        compiler_params=_params(1),
        name="ssm",
    )(vp, m, q, p, a)


def _ssm_params(lam_re, lam_im, log_dt, b_re, b_im, c_re, c_im, d_skip):
    hp = lax.Precision.HIGHEST
    dt = jnp.exp(log_dt)[..., None]
    xr, xi = lam_re * dt, lam_im * dt
    abar_m1_r = jnp.expm1(xr) * jnp.cos(xi) - 2.0 * jnp.square(jnp.sin(0.5 * xi))
    abar_i = jnp.exp(xr) * jnp.sin(xi)
    den = lam_re * lam_re + lam_im * lam_im
    kr = (abar_m1_r * lam_re + abar_i * lam_im) / den
    ki = (abar_i * lam_re - abar_m1_r * lam_im) / den
    bbr = kr[..., None] * b_re - ki[..., None] * b_im
    bbi = kr[..., None] * b_im + ki[..., None] * b_re
    n = jnp.arange(CHUNK + 1, dtype=F32)[:, None, None, None]
    mag = jnp.exp(n * xr[None])
    pr, pi = mag * jnp.cos(n * xi[None]), mag * jnp.sin(n * xi[None])
    wr = c_re[None] * pr[:, :, :, None, :] - c_im[None] * pi[:, :, :, None, :]
    wi = c_re[None] * pi[:, :, :, None, :] + c_im[None] * pr[:, :, :, None, :]
    kern = (jnp.einsum('ndgop,dgpc->ndgoc', wr, bbr, precision=hp)
            - jnp.einsum('ndgop,dgpc->ndgoc', wi, bbi, precision=hp))
    s_idx = jnp.arange(CHUNK)[:, None]
    t_idx = jnp.arange(CHUNK)[None, :]
    lag_f = jnp.clip(t_idx - s_idx, 0, CHUNK)
    lag_b = jnp.clip(s_idx - t_idx, 0, CHUNK)
    kf = jnp.where((s_idx <= t_idx)[:, :, None, None, None], kern[lag_f, 0], 0.0)
    kb = jnp.where((s_idx >= t_idx)[:, :, None, None, None], kern[lag_b, 1], 0.0)
    eye_t = (s_idx == t_idx).astype(F32)[:, :, None, None, None]
    eye_c = jnp.eye(SSM_GROUP, dtype=F32)[None, None, None]
    skip = eye_t * eye_c * d_skip.reshape(SSM_GROUPS, SSM_GROUP)[None, None, :, :, None]
    toep = jnp.transpose(kf + kb + skip, (2, 0, 4, 1, 3))
    toep = toep.reshape(SSM_GROUPS, CHUNK * SSM_GROUP, CHUNK * SSM_GROUP)
    pw_f = jnp.arange(CHUNK - 1, -1, -1)
    pw_b = jnp.arange(CHUNK)

    def in_map(d, pw):
        ar, ai = pr[pw, d], pi[pw, d]
        qr = ar[..., None] * bbr[d][None] - ai[..., None] * bbi[d][None]
        qi = ar[..., None] * bbi[d][None] + ai[..., None] * bbr[d][None]
        f = lambda z: jnp.transpose(z, (1, 0, 3, 2)).reshape(SSM_GROUPS, CHUNK * SSM_GROUP, SSM_STATE)
        return f(qr), f(qi)

    qfr, qfi = in_map(0, pw_f)
    qbr, qbi = in_map(1, pw_b)
    pw_of = jnp.arange(1, CHUNK + 1)
    pw_ob = jnp.arange(CHUNK, 0, -1)

    def out_map(d, pw):
        g = lambda z: jnp.transpose(z[pw, d], (1, 3, 0, 2)).reshape(SSM_GROUPS, SSM_STATE, CHUNK * SSM_GROUP)
        return g(wr), -g(wi)

    pfr, pfi = out_map(0, pw_of)
    pbr, pbi = out_map(1, pw_ob)

    def pair_cols(parts):
        out = jnp.zeros((N_PAIRS, PAIR_W, 4, 2, SSM_STATE), F32)
        for pi_, z in enumerate(parts):
            zz = z.reshape(N_PAIRS, 2, CHUNK * SSM_GROUP, SSM_STATE)
            for gl in range(2):
                out = out.at[:, gl * 256:(gl + 1) * 256, pi_, gl, :].set(zz[:, gl])
        return out.reshape(N_PAIRS, PAIR_W, PAIR_W)

    def pair_rows(parts):
        out = jnp.zeros((N_PAIRS, 4, 2, SSM_STATE, PAIR_W), F32)
        for pi_, z in enumerate(parts):
            zz = z.reshape(N_PAIRS, 2, SSM_STATE, CHUNK * SSM_GROUP)
            for gl in range(2):
                out = out.at[:, pi_, gl, :, gl * 256:(gl + 1) * 256].set(zz[:, gl])
        return out.reshape(N_PAIRS, PAIR_W, PAIR_W)

    q_pair = pair_cols([qfr, qfi, qbr, qbi])
    p_pair = pair_rows([pfr, pfi, pbr, pbi])
    tp = toep.reshape(N_PAIRS, 2, 256, 256)
    m_pair = jnp.zeros((N_PAIRS, PAIR_W, PAIR_W), F32)
    m_pair = m_pair.at[:, 0:256, 0:256].set(tp[:, 0]).at[:, 256:512, 256:512].set(tp[:, 1])
    new = np.arange(PAIR_W)
    tl, gl, c = new // 32, (new // 16) % 2, new % 16
    idx = gl * 256 + tl * 16 + c
    m_pair = m_pair[:, idx][:, :, idx]
    q_pair = q_pair[:, idx, :]
    p_pair = p_pair[:, :, idx]
    dec = jnp.stack([pr[CHUNK, 0], pi[CHUNK, 0], pr[CHUNK, 1], pi[CHUNK, 1]], axis=0)
    dec = jnp.transpose(dec.reshape(4, N_PAIRS, 128), (1, 0, 2))
    dec = jnp.concatenate([dec, jnp.zeros((N_PAIRS, 4, 128), F32)], axis=1)
    return m_pair.astype(BF16), q_pair.astype(BF16), p_pair.astype(BF16), dec


def _banded(q_ref, k_ref, v_ref, o_ref, l_ref, *, seq, n_sub):
    lane = lax.broadcasted_iota(jnp.int32, (QBLK, 128), 1)
    head0 = (lane & 32) == 0
    low64 = lane < 64
    qi = lax.broadcasted_iota(jnp.int32, (2 * QBLK, KBLK), 0) & (QBLK - 1)
    kj = lax.broadcasted_iota(jnp.int32, (2 * QBLK, KBLK), 1)
    rel = kj - qi
    zero = jnp.zeros((QBLK, 128), BF16)
    blocks_per_seq = n_sub // QBLK

    def block(i, carry):
        base = (i // blocks_per_seq) * n_sub
        m0 = (i % blocks_per_seq) * QBLK
        ks = jnp.clip(m0 - HALF_WIN, 0, n_sub - KBLK)
        band = jnp.abs(rel + (ks - m0)) <= HALF_WIN
        qrow = pl.multiple_of(base + m0, QBLK)
        krow = pl.multiple_of(base + ks, HALF_WIN)
        for pi_ in range(2):
            cs = slice(128 * pi_, 128 * (pi_ + 1))
            q2 = q_ref[pl.ds(qrow, QBLK), cs]
            k2 = k_ref[pl.ds(krow, KBLK), cs]
            v2 = v_ref[pl.ds(krow, KBLK), cs]
            qq = jnp.concatenate([jnp.where(head0, q2, zero), jnp.where(head0, zero, q2)], axis=0)
            s = lax.dot_general(qq, k2, (((1,), (1,)), ((), ())), preferred_element_type=F32)
            s = jnp.where(band, s, NEG_INF)
            mx = jnp.max(s, axis=-1, keepdims=True)
            p = jnp.exp(s - mx)
            den = jnp.sum(p, axis=-1, keepdims=True)
            pv = _dot(p.astype(BF16), v2)
            lse = mx + jnp.log(den)
            o_ref[pl.ds(qrow, QBLK), cs] = jnp.where(low64, pv[:QBLK] / den[:QBLK], pv[QBLK:] / den[QBLK:])
            l_ref[pl.ds(qrow, QBLK), cs] = jnp.where(low64, lse[:QBLK], lse[QBLK:])
        return carry

    lax.fori_loop(0, seq // QBLK, block, 0)


def _attn_kernel(q_ref, k_ref, v_ref, perm_ref, out_ref, qd_ref, kd_ref, vd_ref, o_ref, l_ref,
                 num_ref, mx_ref, den_ref, *, seq):
    g = pl.program_id(1)
    n_tiles = seq // PTILE

    def deinterleave(dil, pidx):
        n_loc = PTILE // dil
        n_sub = seq // dil
        pm = perm_ref[pidx]

        def tile(j, carry):
            rows = pl.ds(pl.multiple_of(j * PTILE, PTILE), PTILE)
            for src, dst in ((q_ref, qd_ref), (k_ref, kd_ref), (v_ref, vd_ref)):
                y = _dot(pm, src[rows, :]).astype(BF16)
                for r in range(dil):
                    dst[pl.ds(pl.multiple_of(r * n_sub + j * n_loc, n_loc), n_loc), :] = y[r * n_loc:(r + 1) * n_loc]
            return carry

        lax.fori_loop(0, n_tiles, tile, 0)

    def merge(dil, pidx, first, last):
        n_loc = PTILE // dil
        n_sub = seq // dil

        def tile(j, carry):
            rows = pl.ds(pl.multiple_of(j * PTILE, PTILE), PTILE)
            if dil == 1:
                o, l = o_ref[rows, :], l_ref[rows, :]
            else:
                pm = perm_ref[pidx]

                def natural(ref):
                    t = jnp.concatenate(
                        [ref[pl.ds(pl.multiple_of(r * n_sub + j * n_loc, n_loc), n_loc), :] for r in range(dil)], axis=0)
                    hi = t.astype(BF16)
                    lo = (t - hi.astype(F32)).astype(BF16)
                    return _dot(pm, hi) + _dot(pm, lo)
                o, l = natural(o_ref), natural(l_ref)
            if first:
                num, mx, den = o, l, jnp.ones_like(l)
            else:
                m_old = mx_ref[rows, :]
                mx = jnp.maximum(m_old, l)
                a, b = jnp.exp(m_old - mx), jnp.exp(l - mx)
                num = num_ref[rows, :] * a + o * b
                den = den_ref[rows, :] * a + b
            if last:
                out_ref[rows, :] = (num / den).astype(BF16)
            else:
                num_ref[rows, :] = num
                mx_ref[rows, :] = mx
                den_ref[rows, :] = den
            return carry

        lax.fori_loop(0, n_tiles, tile, 0)

    n_groups = len(ATTN_PATTERNS)
    for gi, (_, dil) in enumerate(ATTN_PATTERNS):
        @pl.when(g == gi)
        def _():
            if dil == 1:
                _banded(q_ref, k_ref, v_ref, o_ref, l_ref, seq=seq, n_sub=seq)
            else:
                deinterleave(dil, 2 * (gi - 1))
                _banded(qd_ref, kd_ref, vd_ref, o_ref, l_ref, seq=seq, n_sub=seq // dil)
            merge(dil, 2 * (gi - 1) + 1, gi == 0, gi == n_groups - 1)


def _attn(q, k, v, perms):
    bsz, seq, _ = q.shape
    w = ATTN_OUT_WIDTH
    blk = pl.BlockSpec((None, seq, w), lambda b, g: (b, 0, g))
    return pl.pallas_call(
        functools.partial(_attn_kernel, seq=seq),
        grid=(bsz, len(ATTN_PATTERNS)),
        in_specs=[blk, blk, blk,
                  pl.BlockSpec(perms.shape, lambda b, g: (0, 0, 0), pipeline_mode=pl.Buffered(1))],
        out_specs=pl.BlockSpec((None, seq, w), lambda b, g: (b, 0, 0)),
        out_shape=jax.ShapeDtypeStruct((bsz, seq, w), BF16),
        scratch_shapes=[pltpu.VMEM((seq, w), BF16)] * 3 + [pltpu.VMEM((seq, w), F32)] * 5,
        compiler_params=_params(2),
        name="attn",
    )(q, k, v, perms)


def _interleave_perms():
    mats = []
    for _, dil in ATTN_PATTERNS[1:]:
        n_loc = PTILE // dil
        p = np.zeros((PTILE, PTILE), np.float32)
        pos = np.arange(PTILE)
        p[(pos % dil) * n_loc + pos // dil, pos] = 1.0
        mats.extend([p, p.T])
    return jnp.asarray(np.stack(mats), dtype=BF16)


def _mix_kernel(x_ref, z_ref, a_ref, perm_ref, wg_ref, bg_ref, wglu_ref, wbr_ref, wout_ref, g_ref, b_ref, h_ref,
                *, bsz, ts):
    tm = bsz * ts
    x = x_ref[...].reshape(tm, D_MODEL)
    xb = x.astype(BF16)
    gates = jax.nn.sigmoid(_dot(xb, wg_ref[...]) + bg_ref[...])
    zs = [[z_ref[p, :, 128 * j:128 * (j + 1)].astype(F32) for j in range(4)] for p in range(N_PAIRS)]
    z = _dot(perm_ref[...], _unplace(zs, tm // CHUNK).astype(BF16)).astype(BF16)
    gv = _dot(z, wglu_ref[...])
    ssm_out = gv[:, :D_MODEL] * jax.nn.sigmoid(gv[:, D_MODEL:])
    attn_out = _dot(a_ref[...].reshape(tm, ATTN_OUT_WIDTH), wbr_ref[...])
    mix = gates[:, :D_MODEL] * ssm_out + gates[:, D_MODEL:] * attn_out
    mixed = _dot(mix.astype(BF16), wout_ref[...])
    h = _layer_norm(DEEPNORM_ALPHA * x + mixed, g_ref[...], b_ref[...])
    h_ref[...] = h.reshape(bsz, ts, D_MODEL)


def _mix(x, z, attn, perm_t, wg, bg, wglu, wbr, wout, g, b, ts):
    bsz, seq, _ = x.shape
    tm = bsz * ts
    tile = lambda w_: pl.BlockSpec((bsz, ts, w_), lambda i: (0, i, 0))
    wspec = lambda shape: pl.BlockSpec(shape, lambda i: (0, 0), pipeline_mode=pl.Buffered(1))
    return pl.pallas_call(
        functools.partial(_mix_kernel, bsz=bsz, ts=ts),
        grid=(seq // ts,),
        in_specs=[tile(D_MODEL), pl.BlockSpec((N_PAIRS, tm // CHUNK, PAIR_W), lambda i: (0, i, 0)),
                  tile(ATTN_OUT_WIDTH), wspec((tm, tm)),
                  wspec((D_MODEL, 2 * D_MODEL)), wspec((1, 2 * D_MODEL)), wspec((SSM_WIDTH, 2 * D_MODEL)),
                  wspec((ATTN_OUT_WIDTH, D_MODEL)), wspec((D_MODEL, D_MODEL)), wspec((1, D_MODEL)), wspec((1, D_MODEL))],
        out_specs=tile(D_MODEL),
        out_shape=jax.ShapeDtypeStruct((bsz, seq, D_MODEL), F32),
        compiler_params=_params(1),
        name="mix",
    )(x, z, attn, perm_t, wg, bg, wglu, wbr, wout, g, b)


FFN_CHUNK = 256


def _ffn_kernel(h_ref, hp_ref, hn_ref, wup_ref, cw_ref, cb_ref, wdn_ref, g_ref, b_ref, o_ref, s_ref,
                *, tm, tiles_per_seq):
    i = pl.program_id(0)
    j = i % tiles_per_seq
    h = h_ref[...]
    hb = h.astype(BF16)
    prev_ok = (j > 0).astype(F32)
    next_ok = (j < tiles_per_seq - 1).astype(F32)
    hpb = (hp_ref[...] * prev_ok).astype(BF16)
    hnb = (hn_ref[...] * next_ok).astype(BF16)
    acc = jnp.zeros((tm, D_MODEL), F32)
    for c in range(D_FF // FFN_CHUNK):
        halves = []
        for c0 in (c * FFN_CHUNK, D_FF + c * FFN_CHUNK):
            w = wup_ref[:, c0:c0 + FFN_CHUNK]
            s_ref[8:8 + tm, :] = _dot(hb, w)
            s_ref[0:8, :] = _dot(hpb, w)
            s_ref[8 + tm:16 + tm, :] = _dot(hnb, w)
            cw = cw_ref[:, c0:c0 + FFN_CHUNK]
            halves.append(s_ref[7:7 + tm, :] * cw[0:1] + s_ref[8:8 + tm, :] * cw[1:2]
                          + s_ref[9:9 + tm, :] * cw[2:3] + cb_ref[:, c0:c0 + FFN_CHUNK])
        a, val = halves
        acc = acc + _dot((_gelu(a) * val).astype(BF16), wdn_ref[c * FFN_CHUNK:(c + 1) * FFN_CHUNK, :])
    o_ref[...] = _layer_norm(DEEPNORM_ALPHA * h + acc, g_ref[...], b_ref[...])


def _ffn(h, wup, cw, cb, wdn, g, b, tm, seq):
    t = h.shape[0]
    tps = seq // tm
    nblk8 = t // 8
    fixed = lambda i: (0, 0)
    wspec = lambda shape: pl.BlockSpec(shape, fixed, pipeline_mode=pl.Buffered(1))
    return pl.pallas_call(
        functools.partial(_ffn_kernel, tm=tm, tiles_per_seq=tps),
        grid=(t // tm,),
        in_specs=[pl.BlockSpec((tm, D_MODEL), lambda i: (i, 0)),
                  pl.BlockSpec((8, D_MODEL), lambda i: (jnp.maximum(i * (tm // 8) - 1, 0), 0)),
                  pl.BlockSpec((8, D_MODEL), lambda i: (jnp.minimum((i + 1) * (tm // 8), nblk8 - 1), 0)),
                  wspec((D_MODEL, 2 * D_FF)), wspec((8, 2 * D_FF)), wspec((1, 2 * D_FF)),
                  wspec((D_FF, D_MODEL)), wspec((1, D_MODEL)), wspec((1, D_MODEL))],
        out_specs=pl.BlockSpec((tm, D_MODEL), lambda i: (i, 0)),
        out_shape=jax.ShapeDtypeStruct((t, D_MODEL), F32),
        scratch_shapes=[pltpu.VMEM((tm + 16, FFN_CHUNK), F32)],
        compiler_params=_params(1),
        name="ffn",
    )(h, h, h, wup, cw, cb, wdn, g, b)


def _qk_column_order():
    cols = []
    for pair in range(N_HEADS // 2):
        for part in range(2):
            for hh in range(2):
                head = 2 * pair + hh
                cols.extend(head * HEAD_DIM + part * 32 + j for j in range(32))
    return np.asarray(cols, dtype=np.int32)


def _step_major_perm(bsz, ts):
    p = np.zeros((bsz * ts, bsz * ts), np.float32)
    b, k, s = np.meshgrid(np.arange(bsz), np.arange(ts // CHUNK), np.arange(CHUNK), indexing="ij")
    p[(s * (ts // CHUNK) + k) * bsz + b, (b * (ts // CHUNK) + k) * CHUNK + s] = 1.0
    return p


def _layer(x, positions, w_in, b_in, lam_re, lam_im, log_dt, b_re, b_im, c_re, c_im, d_skip,
           w_glu_v, w_glu_g, w_attn_br, w_out, ln1_g, ln1_b, w_up, conv_w, conv_b, w_down, ln2_g, ln2_b):
    bsz, seq, _ = x.shape
    t = bsz * seq
    ts = 32
    n_chunks = seq // CHUNK

    perm = _qk_column_order()
    q0, k0, v0, g0 = SSM_WIDTH, SSM_WIDTH + ATTN_WIDTH, SSM_WIDTH + 2 * ATTN_WIDTH, SSM_WIDTH + 3 * ATTN_WIDTH
    scale = HEAD_DIM ** -0.5
    w_proj = jnp.concatenate([w_in[:, :q0], w_in[:, q0:k0][:, perm] * scale, w_in[:, k0:v0][:, perm],
                              w_in[:, v0:g0]], axis=1).astype(BF16)
    b_proj = jnp.concatenate([b_in[:q0], b_in[q0:k0][perm] * scale, b_in[k0:v0][perm], b_in[v0:g0]])[None, :]
    inv_freq = jnp.power(ROPE_THETA, -jnp.arange(32, dtype=F32) * 2.0 / HEAD_DIM)
    invf = jnp.tile(inv_freq, 4)[None, :]
    sgn = jnp.concatenate([-jnp.ones((64,), F32), jnp.ones((64,), F32)])[None, :]
    step_perm = _step_major_perm(bsz, ts)

    vp, q, k, v = _proj(x, positions[:, :, None], w_proj, b_proj, invf, sgn,
                        jnp.asarray(step_perm, dtype=BF16), ts)

    m_pair, q_pair, p_pair, dec = _ssm_params(lam_re, lam_im, log_dt, b_re, b_im, c_re, c_im, d_skip)
    z = _ssm(vp, m_pair, q_pair, p_pair, dec, n_chunks, bsz)

    attn = _attn(q, k, v, _interleave_perms())

    wg = w_in[:, g0:].astype(BF16)
    bg = b_in[g0:][None, :]
    wglu = jnp.concatenate([w_glu_v, w_glu_g], axis=1).astype(BF16)
    h = _mix(x, z, attn, jnp.asarray(step_perm.T, dtype=BF16), wg, bg, wglu, w_attn_br.astype(BF16),
             w_out.astype(BF16), ln1_g[None, :], ln1_b[None, :], ts)

    cw = jnp.concatenate([conv_w, jnp.zeros((5, 2 * D_FF), F32)], axis=0)
    out = _ffn(h.reshape(t, D_MODEL), w_up.astype(BF16), cw, conv_b[None, :], w_down.astype(BF16),
               ln2_g[None, :], ln2_b[None, :], 512, seq)
    return out.reshape(bsz, seq, D_MODEL)


def kernel(x, positions, w_in, b_in, ssm_lam_re, ssm_lam_im, ssm_log_dt, ssm_b_re, ssm_b_im, ssm_c_re, ssm_c_im, ssm_d, w_glu_v, w_glu_g, w_attn_br, w_out, ln1_g, ln1_b, w_up, conv_w, conv_b, w_down, ln2_g, ln2_b):
    h = x
    for layer in range(w_in.shape[0]):
        h = _layer(h, positions, w_in[layer], b_in[layer], ssm_lam_re[layer], ssm_lam_im[layer],
                   ssm_log_dt[layer], ssm_b_re[layer], ssm_b_im[layer], ssm_c_re[layer], ssm_c_im[layer],
                   ssm_d[layer], w_glu_v[layer], w_glu_g[layer], w_attn_br[layer], w_out[layer],
                   ln1_g[layer], ln1_b[layer], w_up[layer], conv_w[layer], conv_b[layer], w_down[layer],
                   ln2_g[layer], ln2_b[layer])
    return h
```

```python
import functools
import math

import numpy as np
import jax
import jax.numpy as jnp
from jax import lax
from jax.experimental import pallas as pl
from jax.experimental.pallas import tpu as pltpu

D_MODEL = 1024
SSM_WIDTH = 512
SSM_GROUP = 16
SSM_GROUPS = 32
SSM_STATE = 64
HEAD_DIM = 64
HEADS_PER_GROUP = 4
ATTN_PATTERNS = ((128, 1), (512, 4), (2048, 16))
N_HEADS = 12
ATTN_WIDTH = 768
ATTN_OUT_WIDTH = 256
D_FF = 2816
LN_EPS = 1e-5
NEG_INF = -1e30
ROPE_THETA = 10000.0
DEEPNORM_ALPHA = 2.0 ** 0.25

CHUNK = 16
PAIR_W = 2 * SSM_GROUP * CHUNK
N_PAIRS = SSM_GROUPS // 2
HALF_WIN = 64
QBLK = 128
KBLK = 256
PTILE = 256
PROJ_W = SSM_WIDTH + 3 * ATTN_WIDTH
VMEM_LIMIT = 56 * 1024 * 1024
F32 = jnp.float32
BF16 = jnp.bfloat16


def _dot(a, b):
    return jnp.dot(a, b, preferred_element_type=F32)


def _gelu(x):
    return 0.5 * x * (1.0 + lax.erf(x * np.float32(math.sqrt(0.5))))


def _layer_norm(r, g, b):
    mu = jnp.mean(r, axis=-1, keepdims=True)
    c = r - mu
    var = jnp.mean(c * c, axis=-1, keepdims=True)
    return c * lax.rsqrt(var + LN_EPS) * g + b


def _params(n_axes):
    return pltpu.CompilerParams(dimension_semantics=("arbitrary",) * n_axes, vmem_limit_bytes=VMEM_LIMIT)


def _place(r, n_rows):
    lane_grp = lax.broadcasted_iota(jnp.int32, (n_rows, 128), 1) // 32
    out = [[None] * 4 for _ in range(N_PAIRS)]
    for j in range(4):
        for q in range(4):
            src = [r[(4 * j + m) * n_rows:(4 * j + m + 1) * n_rows, 128 * q:128 * (q + 1)] for m in range(4)]
            rolled = [[s if sh == 0 else pltpu.roll(s, 32 * sh, 1) for sh in range(4)] for s in src]
            for pp in range(4):
                d = rolled[0][(0 - pp) % 4]
                for m in range(1, 4):
                    d = jnp.where(lane_grp == m, rolled[m][(m - pp) % 4], d)
                out[4 * q + pp][j] = d
    return out


def _unplace(zs, n_rows):
    lane_grp = lax.broadcasted_iota(jnp.int32, (n_rows, 128), 1) // 32
    rows = []
    for j in range(4):
        for m in range(4):
            cols = []
            for q in range(4):
                d = None
                for pp in range(4):
                    s = zs[4 * q + pp][j]
                    sh = (pp - m) % 4
                    rl = s if sh == 0 else pltpu.roll(s, 32 * sh, 1)
                    d = rl if d is None else jnp.where(lane_grp == pp, rl, d)
                cols.append(d)
            rows.append(jnp.concatenate(cols, axis=1))
    return jnp.concatenate(rows, axis=0)


def _proj_kernel(x_ref, pos_ref, w_ref, b_ref, invf_ref, sgn_ref, perm_ref, vp_ref, q_ref, k_ref, v_ref, *, bsz, ts):
    tm = bsz * ts
    xb = x_ref[...].reshape(tm, D_MODEL).astype(BF16)
    ang = pos_ref[...].reshape(tm, 1).astype(F32) * invf_ref[...]
    cos = jnp.cos(ang)
    sin = jnp.sin(ang) * sgn_ref[...]
    u = _dot(xb, w_ref[:, 0:SSM_WIDTH]) + b_ref[:, 0:SSM_WIDTH]
    r = _dot(perm_ref[...], u.astype(BF16))
    placed = _place(r, tm // CHUNK)
    for p in range(N_PAIRS):
        for j in range(4):
            vp_ref[p, :, 128 * j:128 * (j + 1)] = placed[p][j].astype(BF16)
    for dst, c0 in ((q_ref, SSM_WIDTH), (k_ref, SSM_WIDTH + ATTN_WIDTH)):
        t = _dot(xb, w_ref[:, c0:c0 + ATTN_WIDTH]) + b_ref[:, c0:c0 + ATTN_WIDTH]
        for j in range(ATTN_WIDTH // 128):
            tj = t[:, 128 * j:128 * (j + 1)]
            rot = (tj * cos + pltpu.roll(tj, 64, 1) * sin).astype(BF16)
            dst[:, :, 128 * j:128 * (j + 1)] = rot.reshape(bsz, ts, 128)
    c0 = SSM_WIDTH + 2 * ATTN_WIDTH
    v = _dot(xb, w_ref[:, c0:c0 + ATTN_WIDTH]) + b_ref[:, c0:c0 + ATTN_WIDTH]
    v_ref[...] = v.astype(BF16).reshape(bsz, ts, ATTN_WIDTH)


def _proj(x, pos, w, b, invf, sgn, perm, ts):
    bsz, seq, _ = x.shape
    tm = bsz * ts
    rows = tm // CHUNK
    tile = lambda w_: pl.BlockSpec((bsz, ts, w_), lambda i: (0, i, 0))
    fixed = lambda i: (0, 0)
    return pl.pallas_call(
        functools.partial(_proj_kernel, bsz=bsz, ts=ts),
        grid=(seq // ts,),
        in_specs=[tile(D_MODEL), tile(1),
                  pl.BlockSpec((D_MODEL, PROJ_W), fixed, pipeline_mode=pl.Buffered(1)),
                  pl.BlockSpec((1, PROJ_W), fixed),
                  pl.BlockSpec((1, 128), fixed),
                  pl.BlockSpec((1, 128), fixed),
                  pl.BlockSpec((tm, tm), fixed, pipeline_mode=pl.Buffered(1))],
        out_specs=[pl.BlockSpec((N_PAIRS, rows, PAIR_W), lambda i: (0, i, 0)),
                   tile(ATTN_WIDTH), tile(ATTN_WIDTH), tile(ATTN_WIDTH)],
        out_shape=[jax.ShapeDtypeStruct((N_PAIRS, seq // CHUNK * bsz, PAIR_W), BF16),
                   jax.ShapeDtypeStruct((bsz, seq, ATTN_WIDTH), BF16),
                   jax.ShapeDtypeStruct((bsz, seq, ATTN_WIDTH), BF16),
                   jax.ShapeDtypeStruct((bsz, seq, ATTN_WIDTH), BF16)],
        compiler_params=_params(1),
        name="proj",
    )(x, pos, w, b, invf, sgn, perm)


def _ssm_kernel(v_ref, m_ref, q_ref, p_ref, a_ref, z_ref, h_ref, *, n_chunks, bsz):
    v = v_ref[...]
    h_ref[...] = _dot(v, q_ref[...])
    dec = a_ref[...]
    afr, afi, abr, abi = (jnp.broadcast_to(dec[i:i + 1, :], (bsz, 128)) for i in range(4))

    def step(k, carry):
        hfr, hfi, hbr, hbi = carry
        rf = pl.ds(pl.multiple_of(k * bsz, bsz), bsz)
        rb = pl.ds(pl.multiple_of((n_chunks - 1 - k) * bsz, bsz), bsz)
        xfr = h_ref[rf, 0:128]
        xfi = h_ref[rf, 128:256]
        xbr = h_ref[rb, 256:384]
        xbi = h_ref[rb, 384:512]
        h_ref[rf, 0:128] = hfr
        h_ref[rf, 128:256] = hfi
        h_ref[rb, 256:384] = hbr
        h_ref[rb, 384:512] = hbi
        return (afr * hfr - afi * hfi + xfr, afr * hfi + afi * hfr + xfi,
                abr * hbr - abi * hbi + xbr, abr * hbi + abi * hbr + xbi)

    zero = jnp.zeros((bsz, 128), F32)
    lax.fori_loop(0, n_chunks, step, (zero, zero, zero, zero), unroll=4)
    y = _dot(v, m_ref[...]) + _dot(h_ref[...].astype(BF16), p_ref[...])
    z_ref[...] = _gelu(y).astype(BF16)


def _ssm(vp, m, q, p, a, n_chunks, bsz):
    rows = n_chunks * bsz
    blk = lambda shape: pl.BlockSpec((None,) + shape, lambda i: (i, 0, 0))
    return pl.pallas_call(
        functools.partial(_ssm_kernel, n_chunks=n_chunks, bsz=bsz),
        grid=(N_PAIRS,),
        in_specs=[blk((rows, PAIR_W)), blk((PAIR_W, PAIR_W)), blk((PAIR_W, PAIR_W)),
                  blk((PAIR_W, PAIR_W)), blk((8, 128))],
        out_specs=blk((rows, PAIR_W)),
        out_shape=jax.ShapeDtypeStruct((N_PAIRS, rows, PAIR_W), BF16),
        scratch_shapes=[pltpu.VMEM((rows, PAIR_W), F32)],
        compiler_params=_params(1),
        name="ssm",
    )(vp, m, q, p, a)


def _ssm_params(lam_re, lam_im, log_dt, b_re, b_im, c_re, c_im, d_skip):
    hp = lax.Precision.HIGHEST
    dt = jnp.exp(log_dt)[..., None]
    xr, xi = lam_re * dt, lam_im * dt
    abar_m1_r = jnp.expm1(xr) * jnp.cos(xi) - 2.0 * jnp.square(jnp.sin(0.5 * xi))
    abar_i = jnp.exp(xr) * jnp.sin(xi)
    den = lam_re * lam_re + lam_im * lam_im
    kr = (abar_m1_r * lam_re + abar_i * lam_im) / den
    ki = (abar_i * lam_re - abar_m1_r * lam_im) / den
    bbr = kr[..., None] * b_re - ki[..., None] * b_im
    bbi = kr[..., None] * b_im + ki[..., None] * b_re
    n = jnp.arange(CHUNK + 1, dtype=F32)[:, None, None, None]
    mag = jnp.exp(n * xr[None])
    pr, pi = mag * jnp.cos(n * xi[None]), mag * jnp.sin(n * xi[None])
    wr = c_re[None] * pr[:, :, :, None, :] - c_im[None] * pi[:, :, :, None, :]
    wi = c_re[None] * pi[:, :, :, None, :] + c_im[None] * pr[:, :, :, None, :]
    kern = (jnp.einsum('ndgop,dgpc->ndgoc', wr, bbr, precision=hp)
            - jnp.einsum('ndgop,dgpc->ndgoc', wi, bbi, precision=hp))
    s_idx = jnp.arange(CHUNK)[:, None]
    t_idx = jnp.arange(CHUNK)[None, :]
    lag_f = jnp.clip(t_idx - s_idx, 0, CHUNK)
    lag_b = jnp.clip(s_idx - t_idx, 0, CHUNK)
    kf = jnp.where((s_idx <= t_idx)[:, :, None, None, None], kern[lag_f, 0], 0.0)
    kb = jnp.where((s_idx >= t_idx)[:, :, None, None, None], kern[lag_b, 1], 0.0)
    eye_t = (s_idx == t_idx).astype(F32)[:, :, None, None, None]
    eye_c = jnp.eye(SSM_GROUP, dtype=F32)[None, None, None]
    skip = eye_t * eye_c * d_skip.reshape(SSM_GROUPS, SSM_GROUP)[None, None, :, :, None]
    toep = jnp.transpose(kf + kb + skip, (2, 0, 4, 1, 3))
    toep = toep.reshape(SSM_GROUPS, CHUNK * SSM_GROUP, CHUNK * SSM_GROUP)
    pw_f = jnp.arange(CHUNK - 1, -1, -1)
    pw_b = jnp.arange(CHUNK)

    def in_map(d, pw):
        ar, ai = pr[pw, d], pi[pw, d]
        qr = ar[..., None] * bbr[d][None] - ai[..., None] * bbi[d][None]
        qi = ar[..., None] * bbi[d][None] + ai[..., None] * bbr[d][None]
        f = lambda z: jnp.transpose(z, (1, 0, 3, 2)).reshape(SSM_GROUPS, CHUNK * SSM_GROUP, SSM_STATE)
        return f(qr), f(qi)

    qfr, qfi = in_map(0, pw_f)
    qbr, qbi = in_map(1, pw_b)
    pw_of = jnp.arange(1, CHUNK + 1)
    pw_ob = jnp.arange(CHUNK, 0, -1)

    def out_map(d, pw):
        g = lambda z: jnp.transpose(z[pw, d], (1, 3, 0, 2)).reshape(SSM_GROUPS, SSM_STATE, CHUNK * SSM_GROUP)
        return g(wr), -g(wi)

    pfr, pfi = out_map(0, pw_of)
    pbr, pbi = out_map(1, pw_ob)

    def pair_cols(parts):
        out = jnp.zeros((N_PAIRS, PAIR_W, 4, 2, SSM_STATE), F32)
        for pi_, z in enumerate(parts):
            zz = z.reshape(N_PAIRS, 2, CHUNK * SSM_GROUP, SSM_STATE)
            for gl in range(2):
                out = out.at[:, gl * 256:(gl + 1) * 256, pi_, gl, :].set(zz[:, gl])
        return out.reshape(N_PAIRS, PAIR_W, PAIR_W)

    def pair_rows(parts):
        out = jnp.zeros((N_PAIRS, 4, 2, SSM_STATE, PAIR_W), F32)
        for pi_, z in enumerate(parts):
            zz = z.reshape(N_PAIRS, 2, SSM_STATE, CHUNK * SSM_GROUP)
            for gl in range(2):
                out = out.at[:, pi_, gl, :, gl * 256:(gl + 1) * 256].set(zz[:, gl])
        return out.reshape(N_PAIRS, PAIR_W, PAIR_W)

    q_pair = pair_cols([qfr, qfi, qbr, qbi])
    p_pair = pair_rows([pfr, pfi, pbr, pbi])
    tp = toep.reshape(N_PAIRS, 2, 256, 256)
    m_pair = jnp.zeros((N_PAIRS, PAIR_W, PAIR_W), F32)
    m_pair = m_pair.at[:, 0:256, 0:256].set(tp[:, 0]).at[:, 256:512, 256:512].set(tp[:, 1])
    def rows_step_major(z):
        z = z.reshape(N_PAIRS, 2, CHUNK, SSM_GROUP, z.shape[-1])
        return jnp.transpose(z, (0, 2, 1, 3, 4)).reshape(N_PAIRS, PAIR_W, -1)

    def cols_step_major(z):
        z = z.reshape(N_PAIRS, z.shape[1], 2, CHUNK, SSM_GROUP)
        return jnp.transpose(z, (0, 1, 3, 2, 4)).reshape(N_PAIRS, -1, PAIR_W)

    m_pair = cols_step_major(rows_step_major(m_pair))
    q_pair = rows_step_major(q_pair)
    p_pair = cols_step_major(p_pair)
    dec = jnp.stack([pr[CHUNK, 0], pi[CHUNK, 0], pr[CHUNK, 1], pi[CHUNK, 1]], axis=0)
    dec = jnp.transpose(dec.reshape(4, N_PAIRS, 128), (1, 0, 2))
    dec = jnp.concatenate([dec, jnp.zeros((N_PAIRS, 4, 128), F32)], axis=1)
    return m_pair.astype(BF16), q_pair.astype(BF16), p_pair.astype(BF16), dec


def _banded(q_ref, k_ref, v_ref, bias_ref, o_ref, m_ref, d_ref, *, seq, n_sub):
    lane = lax.broadcasted_iota(jnp.int32, (QBLK, 128), 1)
    head0 = (lane & 32) == 0
    low64 = lane < 64
    zero = jnp.zeros((QBLK, 128), BF16)
    ones = jnp.ones((KBLK, 128), BF16)
    blocks_per_seq = n_sub // QBLK

    def block(i, carry):
        base = (i // blocks_per_seq) * n_sub
        m0 = (i % blocks_per_seq) * QBLK
        ks = jnp.clip(m0 - HALF_WIN, 0, n_sub - KBLK)
        bias = bias_ref[(m0 - ks) // HALF_WIN]
        qrow = pl.multiple_of(base + m0, QBLK)
        krow = pl.multiple_of(base + ks, HALF_WIN)
        for pi_ in range(2):
            cs = slice(128 * pi_, 128 * (pi_ + 1))
            q2 = q_ref[pl.ds(qrow, QBLK), cs]
            k2 = k_ref[pl.ds(krow, KBLK), cs]
            v2 = jnp.concatenate([v_ref[pl.ds(krow, KBLK), cs], ones], axis=1)
            qq = jnp.concatenate([jnp.where(head0, q2, zero), jnp.where(head0, zero, q2)], axis=0)
            s = lax.dot_general(qq, k2, (((1,), (1,)), ((), ())), preferred_element_type=F32)
            s = s + bias
            mx = jnp.max(s, axis=-1, keepdims=True)
            pv = _dot(jnp.exp(s - mx).astype(BF16), v2)
            rows = pl.ds(qrow, QBLK)
            o_ref[rows, cs] = jnp.where(low64, pv[:QBLK, :128], pv[QBLK:, :128]).astype(o_ref.dtype)
            m_ref[rows, cs] = jnp.where(low64, mx[:QBLK], mx[QBLK:])
            d_ref[rows, cs] = jnp.where(low64, pv[:QBLK, 128:], pv[QBLK:, 128:]).astype(d_ref.dtype)
        return carry

    lax.fori_loop(0, seq // QBLK, block, 0, unroll=16)


def _attn_kernel(q_ref, k_ref, v_ref, perm_ref, bias_ref, out_ref, qd_ref, kd_ref, vd_ref, od_ref, dd_ref, md_ref,
                 num_ref, mx_ref, den_ref, *, seq):
    g = pl.program_id(1)
    n_tiles = seq // PTILE

    def deinterleave(dil, pidx):
        n_loc = PTILE // dil
        n_sub = seq // dil
        pm = perm_ref[pidx]

        def tile(j, carry):
            rows = pl.ds(pl.multiple_of(j * PTILE, PTILE), PTILE)
            for src, dst in ((q_ref, qd_ref), (k_ref, kd_ref), (v_ref, vd_ref)):
                y = _dot(pm, src[rows, :]).astype(BF16)
                for r in range(dil):
                    dst[pl.ds(pl.multiple_of(r * n_sub + j * n_loc, n_loc), n_loc), :] = y[r * n_loc:(r + 1) * n_loc]
            return carry

        lax.fori_loop(0, n_tiles, tile, 0)

    def merge(dil, pidx, last):
        n_loc = PTILE // dil
        n_sub = seq // dil
        pm = perm_ref[pidx]

        def tile(j, carry):
            rows = pl.ds(pl.multiple_of(j * PTILE, PTILE), PTILE)

            def gathered(ref):
                return jnp.concatenate(
                    [ref[pl.ds(pl.multiple_of(r * n_sub + j * n_loc, n_loc), n_loc), :] for r in range(dil)], axis=0)

            o = _dot(pm, gathered(od_ref))
            d_new = _dot(pm, gathered(dd_ref))
            t = gathered(md_ref)
            hi = t.astype(BF16)
            lo = (t - hi.astype(F32)).astype(BF16)
            m_new = _dot(pm, hi) + _dot(pm, lo)
            m_old = mx_ref[rows, :]
            mx = jnp.maximum(m_old, m_new)
            a, b = jnp.exp(m_old - mx), jnp.exp(m_new - mx)
            num = num_ref[rows, :] * a + o * b
            den = den_ref[rows, :] * a + d_new * b
            if last:
                out_ref[rows, :] = (num / den).astype(BF16)
            else:
                num_ref[rows, :] = num
                mx_ref[rows, :] = mx
                den_ref[rows, :] = den
            return carry

        lax.fori_loop(0, n_tiles, tile, 0)

    n_groups = len(ATTN_PATTERNS)
    for gi, (_, dil) in enumerate(ATTN_PATTERNS):
        @pl.when(g == gi)
        def _():
            if gi == 0:
                _banded(q_ref, k_ref, v_ref, bias_ref, num_ref, mx_ref, den_ref, seq=seq, n_sub=seq)
            else:
                deinterleave(dil, 2 * (gi - 1))
                _banded(qd_ref, kd_ref, vd_ref, bias_ref, od_ref, md_ref, dd_ref, seq=seq, n_sub=seq // dil)
                merge(dil, 2 * (gi - 1) + 1, gi == n_groups - 1)


def _band_bias():
    r = np.arange(2 * QBLK)[:, None] & (QBLK - 1)
    j = np.arange(KBLK)[None, :]
    return jnp.asarray(np.stack([np.where(np.abs(j - HALF_WIN * i - r) <= HALF_WIN, 0.0, NEG_INF)
                                 for i in range(3)]), dtype=F32)


def _attn(q, k, v, perms, bias):
    bsz, seq, _ = q.shape
    w = ATTN_OUT_WIDTH
    blk = pl.BlockSpec((None, seq, w), lambda b, g: (b, 0, g))
    return pl.pallas_call(
        functools.partial(_attn_kernel, seq=seq),
        grid=(bsz, len(ATTN_PATTERNS)),
        in_specs=[blk, blk, blk,
                  pl.BlockSpec(perms.shape, lambda b, g: (0, 0, 0), pipeline_mode=pl.Buffered(1)),
                  pl.BlockSpec(bias.shape, lambda b, g: (0, 0, 0), pipeline_mode=pl.Buffered(1))],
        out_specs=pl.BlockSpec((None, seq, w), lambda b, g: (b, 0, 0)),
        out_shape=jax.ShapeDtypeStruct((bsz, seq, w), BF16),
        scratch_shapes=[pltpu.VMEM((seq, w), BF16)] * 5 + [pltpu.VMEM((seq, w), F32)] * 4,
        compiler_params=_params(2),
        name="attn",
    )(q, k, v, perms, bias)


def _interleave_perms():
    mats = []
    for _, dil in ATTN_PATTERNS[1:]:
        n_loc = PTILE // dil
        p = np.zeros((PTILE, PTILE), np.float32)
        pos = np.arange(PTILE)
        p[(pos % dil) * n_loc + pos // dil, pos] = 1.0
        mats.extend([p, p.T])
    return jnp.asarray(np.stack(mats), dtype=BF16)


def _mix_kernel(x_ref, z_ref, a_ref, perm_ref, wg_ref, bg_ref, wglu_ref, wbr_ref, wout_ref, g_ref, b_ref, h_ref,
                *, bsz, ts):
    tm = bsz * ts
    x = x_ref[...].reshape(tm, D_MODEL)
    xb = x.astype(BF16)
    gates = jax.nn.sigmoid(_dot(xb, wg_ref[...]) + bg_ref[...])
    zs = [[z_ref[p, :, 128 * j:128 * (j + 1)].astype(F32) for j in range(4)] for p in range(N_PAIRS)]
    z = _dot(perm_ref[...], _unplace(zs, tm // CHUNK).astype(BF16)).astype(BF16)
    gv = _dot(z, wglu_ref[...])
    ssm_out = gv[:, :D_MODEL] * jax.nn.sigmoid(gv[:, D_MODEL:])
    attn_out = _dot(a_ref[...].reshape(tm, ATTN_OUT_WIDTH), wbr_ref[...])
    mix = gates[:, :D_MODEL] * ssm_out + gates[:, D_MODEL:] * attn_out
    mixed = _dot(mix.astype(BF16), wout_ref[...])
    h = _layer_norm(DEEPNORM_ALPHA * x + mixed, g_ref[...], b_ref[...])
    h_ref[...] = h.reshape(bsz, ts, D_MODEL)


def _mix(x, z, attn, perm_t, wg, bg, wglu, wbr, wout, g, b, ts):
    bsz, seq, _ = x.shape
    tm = bsz * ts
    tile = lambda w_: pl.BlockSpec((bsz, ts, w_), lambda i: (0, i, 0))
    wspec = lambda shape: pl.BlockSpec(shape, lambda i: (0, 0), pipeline_mode=pl.Buffered(1))
    return pl.pallas_call(
        functools.partial(_mix_kernel, bsz=bsz, ts=ts),
        grid=(seq // ts,),
        in_specs=[tile(D_MODEL), pl.BlockSpec((N_PAIRS, tm // CHUNK, PAIR_W), lambda i: (0, i, 0)),
                  tile(ATTN_OUT_WIDTH), wspec((tm, tm)),
                  wspec((D_MODEL, 2 * D_MODEL)), wspec((1, 2 * D_MODEL)), wspec((SSM_WIDTH, 2 * D_MODEL)),
                  wspec((ATTN_OUT_WIDTH, D_MODEL)), wspec((D_MODEL, D_MODEL)), wspec((1, D_MODEL)), wspec((1, D_MODEL))],
        out_specs=tile(D_MODEL),
        out_shape=jax.ShapeDtypeStruct((bsz, seq, D_MODEL), F32),
        compiler_params=_params(1),
        name="mix",
    )(x, z, attn, perm_t, wg, bg, wglu, wbr, wout, g, b)


FFN_CHUNK = 256
FFN_SHIFT_BUFS = 4
FFN_TM = 1024


def _ffn_kernel(h_ref, hp_ref, hn_ref, wup_ref, cw_ref, cb_ref, wdn_ref, g_ref, b_ref, o_ref, s_ref, act_ref,
                *, tm, tiles_per_seq):
    i = pl.program_id(0)
    j = i % tiles_per_seq
    h = h_ref[...]
    hb = h.astype(BF16)
    prev_ok = (j > 0).astype(F32)
    next_ok = (j < tiles_per_seq - 1).astype(F32)
    hpb = (hp_ref[...] * prev_ok).astype(BF16)
    hnb = (hn_ref[...] * next_ok).astype(BF16)
    for c in range(D_FF // FFN_CHUNK):
        halves = []
        for half, c0 in enumerate((c * FFN_CHUNK, D_FF + c * FFN_CHUNK)):
            buf = s_ref.at[(2 * c + half) % FFN_SHIFT_BUFS]
            w = wup_ref[:, c0:c0 + FFN_CHUNK]
            buf[8:8 + tm, :] = _dot(hb, w)
            buf[0:8, :] = _dot(hpb, w)
            buf[8 + tm:16 + tm, :] = _dot(hnb, w)
            cw = cw_ref[:, c0:c0 + FFN_CHUNK]
            halves.append(buf[7:7 + tm, :] * cw[0:1] + buf[8:8 + tm, :] * cw[1:2]
                          + buf[9:9 + tm, :] * cw[2:3] + cb_ref[:, c0:c0 + FFN_CHUNK])
        a, val = halves
        act_ref[:, c * FFN_CHUNK:(c + 1) * FFN_CHUNK] = (_gelu(a) * val).astype(BF16)
    ffn = _dot(act_ref[...], wdn_ref[...])
    o_ref[...] = _layer_norm(DEEPNORM_ALPHA * h + ffn, g_ref[...], b_ref[...])


def _ffn(h, wup, cw, cb, wdn, g, b, tm, seq):
    t = h.shape[0]
    tps = seq // tm
    nblk8 = t // 8
    fixed = lambda i: (0, 0)
    wspec = lambda shape: pl.BlockSpec(shape, fixed, pipeline_mode=pl.Buffered(1))
    return pl.pallas_call(
        functools.partial(_ffn_kernel, tm=tm, tiles_per_seq=tps),
        grid=(t // tm,),
        in_specs=[pl.BlockSpec((tm, D_MODEL), lambda i: (i, 0)),
                  pl.BlockSpec((8, D_MODEL), lambda i: (jnp.maximum(i * (tm // 8) - 1, 0), 0)),
                  pl.BlockSpec((8, D_MODEL), lambda i: (jnp.minimum((i + 1) * (tm // 8), nblk8 - 1), 0)),
                  wspec((D_MODEL, 2 * D_FF)), wspec((8, 2 * D_FF)), wspec((1, 2 * D_FF)),
                  wspec((D_FF, D_MODEL)), wspec((1, D_MODEL)), wspec((1, D_MODEL))],
        out_specs=pl.BlockSpec((tm, D_MODEL), lambda i: (i, 0)),
        out_shape=jax.ShapeDtypeStruct((t, D_MODEL), F32),
        scratch_shapes=[pltpu.VMEM((FFN_SHIFT_BUFS, tm + 16, FFN_CHUNK), F32), pltpu.VMEM((tm, D_FF), BF16)],
        compiler_params=_params(1),
        name="ffn",
    )(h, h, h, wup, cw, cb, wdn, g, b)


def _pair_half_major(a):
    lead = a.shape[:-1]
    a = a.reshape(lead + (N_HEADS // 2, 2, 2, 32))
    return jnp.swapaxes(a, -3, -2).reshape(lead + (ATTN_WIDTH,))


def _step_major_perm(bsz, ts):
    p = np.zeros((bsz * ts, bsz * ts), np.float32)
    b, k, s = np.meshgrid(np.arange(bsz), np.arange(ts // CHUNK), np.arange(CHUNK), indexing="ij")
    p[(s * (ts // CHUNK) + k) * bsz + b, (b * (ts // CHUNK) + k) * CHUNK + s] = 1.0
    return p


def _layer(x, positions, w_in, b_in, lam_re, lam_im, log_dt, b_re, b_im, c_re, c_im, d_skip,
           w_glu_v, w_glu_g, w_attn_br, w_out, ln1_g, ln1_b, w_up, conv_w, conv_b, w_down, ln2_g, ln2_b):
    bsz, seq, _ = x.shape
    t = bsz * seq
    ts = 32
    n_chunks = seq // CHUNK

    q0, k0, v0, g0 = SSM_WIDTH, SSM_WIDTH + ATTN_WIDTH, SSM_WIDTH + 2 * ATTN_WIDTH, SSM_WIDTH + 3 * ATTN_WIDTH
    scale = HEAD_DIM ** -0.5
    w_proj = jnp.concatenate([w_in[:, :q0], _pair_half_major(w_in[:, q0:k0]) * scale,
                              _pair_half_major(w_in[:, k0:v0]), w_in[:, v0:g0]], axis=1).astype(BF16)
    b_proj = jnp.concatenate([b_in[:q0], _pair_half_major(b_in[q0:k0]) * scale, _pair_half_major(b_in[k0:v0]),
                              b_in[v0:g0]])[None, :]
    inv_freq = jnp.power(ROPE_THETA, -jnp.arange(32, dtype=F32) * 2.0 / HEAD_DIM)
    invf = jnp.tile(inv_freq, 4)[None, :]
    sgn = jnp.concatenate([-jnp.ones((64,), F32), jnp.ones((64,), F32)])[None, :]
    step_perm = _step_major_perm(bsz, ts)

    vp, q, k, v = _proj(x, positions[:, :, None], w_proj, b_proj, invf, sgn,
                        jnp.asarray(step_perm, dtype=BF16), ts)

    m_pair, q_pair, p_pair, dec = _ssm_params(lam_re, lam_im, log_dt, b_re, b_im, c_re, c_im, d_skip)
    z = _ssm(vp, m_pair, q_pair, p_pair, dec, n_chunks, bsz)

    attn = _attn(q, k, v, _interleave_perms(), _band_bias())

    wg = w_in[:, g0:].astype(BF16)
    bg = b_in[g0:][None, :]
    wglu = jnp.concatenate([w_glu_v, w_glu_g], axis=1).astype(BF16)
    h = _mix(x, z, attn, jnp.asarray(step_perm.T, dtype=BF16), wg, bg, wglu, w_attn_br.astype(BF16),
             w_out.astype(BF16), ln1_g[None, :], ln1_b[None, :], ts)

    cw = jnp.concatenate([conv_w, jnp.zeros((5, 2 * D_FF), F32)], axis=0)
    out = _ffn(h.reshape(t, D_MODEL), w_up.astype(BF16), cw, conv_b[None, :], w_down.astype(BF16),
               ln2_g[None, :], ln2_b[None, :], FFN_TM, seq)
    return out.reshape(bsz, seq, D_MODEL)


def kernel(x, positions, w_in, b_in, ssm_lam_re, ssm_lam_im, ssm_log_dt, ssm_b_re, ssm_b_im, ssm_c_re, ssm_c_im, ssm_d, w_glu_v, w_glu_g, w_attn_br, w_out, ln1_g, ln1_b, w_up, conv_w, conv_b, w_down, ln2_g, ln2_b):
    h = x
    for layer in range(w_in.shape[0]):
        h = _layer(h, positions, w_in[layer], b_in[layer], ssm_lam_re[layer], ssm_lam_im[layer],
                   ssm_log_dt[layer], ssm_b_re[layer], ssm_b_im[layer], ssm_c_re[layer], ssm_c_im[layer],
                   ssm_d[layer], w_glu_v[layer], w_glu_g[layer], w_attn_br[layer], w_out[layer],
                   ln1_g[layer], ln1_b[layer], w_up[layer], conv_w[layer], conv_b[layer], w_down[layer],
                   ln2_g[layer], ln2_b[layer])
    return h
```

```python
import functools
import math

import numpy as np
import jax
import jax.numpy as jnp
from jax import lax
from jax.experimental import pallas as pl
from jax.experimental.pallas import tpu as pltpu

D_MODEL = 1024
SSM_WIDTH = 512
SSM_GROUP = 16
SSM_GROUPS = 32
SSM_STATE = 64
HEAD_DIM = 64
HEADS_PER_GROUP = 4
ATTN_PATTERNS = ((128, 1), (512, 4), (2048, 16))
N_HEADS = 12
ATTN_WIDTH = 768
ATTN_OUT_WIDTH = 256
D_FF = 2816
LN_EPS = 1e-5
NEG_INF = -1e30
ROPE_THETA = 10000.0
DEEPNORM_ALPHA = 2.0 ** 0.25

CHUNK = 16
PAIR_W = 2 * SSM_GROUP * CHUNK
N_PAIRS = SSM_GROUPS // 2
HALF_WIN = 64
QBLK = 128
KBLK = 256
PTILE = 256
PROJ_W = SSM_WIDTH + 3 * ATTN_WIDTH
VMEM_LIMIT = 56 * 1024 * 1024
F32 = jnp.float32
BF16 = jnp.bfloat16


def _dot(a, b):
    return jnp.dot(a, b, preferred_element_type=F32)


def _gelu(x):
    return 0.5 * x * (1.0 + lax.erf(x * np.float32(math.sqrt(0.5))))


def _layer_norm(r, g, b):
    mu = jnp.mean(r, axis=-1, keepdims=True)
    c = r - mu
    var = jnp.mean(c * c, axis=-1, keepdims=True)
    return c * lax.rsqrt(var + LN_EPS) * g + b


def _params(n_axes):
    return pltpu.CompilerParams(dimension_semantics=("arbitrary",) * n_axes, vmem_limit_bytes=VMEM_LIMIT)


def _place(r, n_rows):
    lane_grp = lax.broadcasted_iota(jnp.int32, (n_rows, 128), 1) // 32
    out = [[None] * 4 for _ in range(N_PAIRS)]
    for j in range(4):
        for q in range(4):
            src = [r[(4 * j + m) * n_rows:(4 * j + m + 1) * n_rows, 128 * q:128 * (q + 1)] for m in range(4)]
            rolled = [[s if sh == 0 else pltpu.roll(s, 32 * sh, 1) for sh in range(4)] for s in src]
            for pp in range(4):
                d = rolled[0][(0 - pp) % 4]
                for m in range(1, 4):
                    d = jnp.where(lane_grp == m, rolled[m][(m - pp) % 4], d)
                out[4 * q + pp][j] = d
    return out


def _unplace(zs, n_rows):
    lane_grp = lax.broadcasted_iota(jnp.int32, (n_rows, 128), 1) // 32
    rows = []
    for j in range(4):
        for m in range(4):
            cols = []
            for q in range(4):
                d = None
                for pp in range(4):
                    s = zs[4 * q + pp][j]
                    sh = (pp - m) % 4
                    rl = s if sh == 0 else pltpu.roll(s, 32 * sh, 1)
                    d = rl if d is None else jnp.where(lane_grp == pp, rl, d)
                cols.append(d)
            rows.append(jnp.concatenate(cols, axis=1))
    return jnp.concatenate(rows, axis=0)


def _proj_kernel(x_ref, pos_ref, w_ref, b_ref, invf_ref, sgn_ref, perm_ref, vp_ref, q_ref, k_ref, v_ref, *, bsz, ts):
    tm = bsz * ts
    xb = x_ref[...].reshape(tm, D_MODEL).astype(BF16)
    ang = pos_ref[...].reshape(tm, 1).astype(F32) * invf_ref[...]
    cos = jnp.cos(ang)
    sin = jnp.sin(ang) * sgn_ref[...]
    low_half = (lax.broadcasted_iota(jnp.int32, (tm, 128), 1) & 32) == 0
    u = _dot(xb, w_ref[:, 0:SSM_WIDTH]) + b_ref[:, 0:SSM_WIDTH]
    r = _dot(perm_ref[...], u.astype(BF16))
    placed = _place(r, tm // CHUNK)
    for p in range(N_PAIRS):
        for j in range(4):
            vp_ref[p, :, 128 * j:128 * (j + 1)] = placed[p][j].astype(BF16)
    for dst, c0 in ((q_ref, SSM_WIDTH), (k_ref, SSM_WIDTH + ATTN_WIDTH)):
        t = _dot(xb, w_ref[:, c0:c0 + ATTN_WIDTH]) + b_ref[:, c0:c0 + ATTN_WIDTH]
        for j in range(ATTN_WIDTH // 128):
            tj = t[:, 128 * j:128 * (j + 1)]
            partner = jnp.where(low_half, pltpu.roll(tj, 96, 1), pltpu.roll(tj, 32, 1))
            dst[:, :, 128 * j:128 * (j + 1)] = (tj * cos + partner * sin).astype(BF16).reshape(bsz, ts, 128)
    c0 = SSM_WIDTH + 2 * ATTN_WIDTH
    v = _dot(xb, w_ref[:, c0:c0 + ATTN_WIDTH]) + b_ref[:, c0:c0 + ATTN_WIDTH]
    v_ref[...] = v.astype(BF16).reshape(bsz, ts, ATTN_WIDTH)


def _proj(x, pos, w, b, invf, sgn, perm, ts):
    bsz, seq, _ = x.shape
    tm = bsz * ts
    rows = tm // CHUNK
    tile = lambda w_: pl.BlockSpec((bsz, ts, w_), lambda i: (0, i, 0))
    fixed = lambda i: (0, 0)
    return pl.pallas_call(
        functools.partial(_proj_kernel, bsz=bsz, ts=ts),
        grid=(seq // ts,),
        in_specs=[tile(D_MODEL), tile(1),
                  pl.BlockSpec((D_MODEL, PROJ_W), fixed, pipeline_mode=pl.Buffered(1)),
                  pl.BlockSpec((1, PROJ_W), fixed),
                  pl.BlockSpec((1, 128), fixed),
                  pl.BlockSpec((1, 128), fixed),
                  pl.BlockSpec((tm, tm), fixed, pipeline_mode=pl.Buffered(1))],
        out_specs=[pl.BlockSpec((N_PAIRS, rows, PAIR_W), lambda i: (0, i, 0)),
                   tile(ATTN_WIDTH), tile(ATTN_WIDTH), tile(ATTN_WIDTH)],
        out_shape=[jax.ShapeDtypeStruct((N_PAIRS, seq // CHUNK * bsz, PAIR_W), BF16),
                   jax.ShapeDtypeStruct((bsz, seq, ATTN_WIDTH), BF16),
                   jax.ShapeDtypeStruct((bsz, seq, ATTN_WIDTH), BF16),
                   jax.ShapeDtypeStruct((bsz, seq, ATTN_WIDTH), BF16)],
        compiler_params=_params(1),
        name="proj",
    )(x, pos, w, b, invf, sgn, perm)


def _ssm_kernel(v_ref, m_ref, q_ref, p_ref, a_ref, z_ref, h_ref, *, n_chunks, bsz):
    v = v_ref[...]
    h_ref[...] = _dot(v, q_ref[...])
    dec = a_ref[...]
    afr, afi, abr, abi = (jnp.broadcast_to(dec[i:i + 1, :], (bsz, 128)) for i in range(4))

    def step(k, carry):
        hfr, hfi, hbr, hbi = carry
        rf = pl.ds(pl.multiple_of(k * bsz, bsz), bsz)
        rb = pl.ds(pl.multiple_of((n_chunks - 1 - k) * bsz, bsz), bsz)
        xfr = h_ref[rf, 0:128]
        xfi = h_ref[rf, 128:256]
        xbr = h_ref[rb, 256:384]
        xbi = h_ref[rb, 384:512]
        h_ref[rf, 0:128] = hfr
        h_ref[rf, 128:256] = hfi
        h_ref[rb, 256:384] = hbr
        h_ref[rb, 384:512] = hbi
        return (afr * hfr - afi * hfi + xfr, afr * hfi + afi * hfr + xfi,
                abr * hbr - abi * hbi + xbr, abr * hbi + abi * hbr + xbi)

    zero = jnp.zeros((bsz, 128), F32)
    lax.fori_loop(0, n_chunks, step, (zero, zero, zero, zero), unroll=4)
    y = _dot(v, m_ref[...]) + _dot(h_ref[...].astype(BF16), p_ref[...])
    z_ref[...] = _gelu(y).astype(BF16)


def _ssm(vp, m, q, p, a, n_chunks, bsz):
    rows = n_chunks * bsz
    blk = lambda shape: pl.BlockSpec((None,) + shape, lambda i: (i, 0, 0))
    return pl.pallas_call(
        functools.partial(_ssm_kernel, n_chunks=n_chunks, bsz=bsz),
        grid=(N_PAIRS,),
        in_specs=[blk((rows, PAIR_W)), blk((PAIR_W, PAIR_W)), blk((PAIR_W, PAIR_W)),
                  blk((PAIR_W, PAIR_W)), blk((8, 128))],
        out_specs=blk((rows, PAIR_W)),
        out_shape=jax.ShapeDtypeStruct((N_PAIRS, rows, PAIR_W), BF16),
        scratch_shapes=[pltpu.VMEM((rows, PAIR_W), F32)],
        compiler_params=_params(1),
        name="ssm",
    )(vp, m, q, p, a)


def _ssm_params(lam_re, lam_im, log_dt, b_re, b_im, c_re, c_im, d_skip):
    hp = lax.Precision.HIGHEST
    dt = jnp.exp(log_dt)[..., None]
    xr, xi = lam_re * dt, lam_im * dt
    abar_m1_r = jnp.expm1(xr) * jnp.cos(xi) - 2.0 * jnp.square(jnp.sin(0.5 * xi))
    abar_i = jnp.exp(xr) * jnp.sin(xi)
    den = lam_re * lam_re + lam_im * lam_im
    kr = (abar_m1_r * lam_re + abar_i * lam_im) / den
    ki = (abar_i * lam_re - abar_m1_r * lam_im) / den
    bbr = kr[..., None] * b_re - ki[..., None] * b_im
    bbi = kr[..., None] * b_im + ki[..., None] * b_re
    n = jnp.arange(CHUNK + 1, dtype=F32)[:, None, None, None]
    mag = jnp.exp(n * xr[None])
    pr, pi = mag * jnp.cos(n * xi[None]), mag * jnp.sin(n * xi[None])
    wr = c_re[None] * pr[:, :, :, None, :] - c_im[None] * pi[:, :, :, None, :]
    wi = c_re[None] * pi[:, :, :, None, :] + c_im[None] * pr[:, :, :, None, :]
    kern = jnp.einsum('ndgop,dgpc->ndgoc', jnp.concatenate([wr, -wi], axis=-1),
                      jnp.concatenate([bbr, bbi], axis=-2), precision=hp)
    s_idx = jnp.arange(CHUNK)[:, None]
    t_idx = jnp.arange(CHUNK)[None, :]
    lag_f = jnp.clip(t_idx - s_idx, 0, CHUNK)
    lag_b = jnp.clip(s_idx - t_idx, 0, CHUNK)
    kf = jnp.where((s_idx <= t_idx)[:, :, None, None, None], kern[lag_f, 0], 0.0)
    kb = jnp.where((s_idx >= t_idx)[:, :, None, None, None], kern[lag_b, 1], 0.0)
    eye_t = (s_idx == t_idx).astype(F32)[:, :, None, None, None]
    eye_c = jnp.eye(SSM_GROUP, dtype=F32)[None, None, None]
    skip = eye_t * eye_c * d_skip.reshape(SSM_GROUPS, SSM_GROUP)[None, None, :, :, None]
    toep = jnp.transpose(kf + kb + skip, (2, 0, 4, 1, 3))
    toep = toep.reshape(SSM_GROUPS, CHUNK * SSM_GROUP, CHUNK * SSM_GROUP)
    pw_f = jnp.arange(CHUNK - 1, -1, -1)
    pw_b = jnp.arange(CHUNK)

    def in_map(d, pw):
        ar, ai = pr[pw, d], pi[pw, d]
        qr = ar[..., None] * bbr[d][None] - ai[..., None] * bbi[d][None]
        qi = ar[..., None] * bbi[d][None] + ai[..., None] * bbr[d][None]
        f = lambda z: jnp.transpose(z, (1, 0, 3, 2)).reshape(SSM_GROUPS, CHUNK * SSM_GROUP, SSM_STATE)
        return f(qr), f(qi)

    qfr, qfi = in_map(0, pw_f)
    qbr, qbi = in_map(1, pw_b)
    pw_of = jnp.arange(1, CHUNK + 1)
    pw_ob = jnp.arange(CHUNK, 0, -1)

    def out_map(d, pw):
        g = lambda z: jnp.transpose(z[pw, d], (1, 3, 0, 2)).reshape(SSM_GROUPS, SSM_STATE, CHUNK * SSM_GROUP)
        return g(wr), -g(wi)

    pfr, pfi = out_map(0, pw_of)
    pbr, pbi = out_map(1, pw_ob)

    def pair_cols(parts):
        out = jnp.zeros((N_PAIRS, PAIR_W, 4, 2, SSM_STATE), F32)
        for pi_, z in enumerate(parts):
            zz = z.reshape(N_PAIRS, 2, CHUNK * SSM_GROUP, SSM_STATE)
            for gl in range(2):
                out = out.at[:, gl * 256:(gl + 1) * 256, pi_, gl, :].set(zz[:, gl])
        return out.reshape(N_PAIRS, PAIR_W, PAIR_W)

    def pair_rows(parts):
        out = jnp.zeros((N_PAIRS, 4, 2, SSM_STATE, PAIR_W), F32)
        for pi_, z in enumerate(parts):
            zz = z.reshape(N_PAIRS, 2, SSM_STATE, CHUNK * SSM_GROUP)
            for gl in range(2):
                out = out.at[:, pi_, gl, :, gl * 256:(gl + 1) * 256].set(zz[:, gl])
        return out.reshape(N_PAIRS, PAIR_W, PAIR_W)

    q_pair = pair_cols([qfr, qfi, qbr, qbi])
    p_pair = pair_rows([pfr, pfi, pbr, pbi])
    tp = toep.reshape(N_PAIRS, 2, 256, 256)
    m_pair = jnp.zeros((N_PAIRS, PAIR_W, PAIR_W), F32)
    m_pair = m_pair.at[:, 0:256, 0:256].set(tp[:, 0]).at[:, 256:512, 256:512].set(tp[:, 1])
    def rows_step_major(z):
        z = z.reshape(N_PAIRS, 2, CHUNK, SSM_GROUP, z.shape[-1])
        return jnp.transpose(z, (0, 2, 1, 3, 4)).reshape(N_PAIRS, PAIR_W, -1)

    def cols_step_major(z):
        z = z.reshape(N_PAIRS, z.shape[1], 2, CHUNK, SSM_GROUP)
        return jnp.transpose(z, (0, 1, 3, 2, 4)).reshape(N_PAIRS, -1, PAIR_W)

    m_pair = cols_step_major(rows_step_major(m_pair))
    q_pair = rows_step_major(q_pair)
    p_pair = cols_step_major(p_pair)
    dec = jnp.stack([pr[CHUNK, 0], pi[CHUNK, 0], pr[CHUNK, 1], pi[CHUNK, 1]], axis=0)
    dec = jnp.transpose(dec.reshape(4, N_PAIRS, 128), (1, 0, 2))
    dec = jnp.concatenate([dec, jnp.zeros((N_PAIRS, 4, 128), F32)], axis=1)
    return m_pair.astype(BF16), q_pair.astype(BF16), p_pair.astype(BF16), dec


def _banded(q_ref, k_ref, v_ref, bias_ref, o_ref, m_ref, d_ref, *, seq, n_sub):
    lane = lax.broadcasted_iota(jnp.int32, (QBLK, 128), 1)
    low64 = lane < 64
    head0 = low64
    zero = jnp.zeros((QBLK, 128), BF16)
    ones = jnp.ones((KBLK, 128), BF16)
    blocks_per_seq = n_sub // QBLK

    def block(i, carry):
        base = (i // blocks_per_seq) * n_sub
        m0 = (i % blocks_per_seq) * QBLK
        ks = jnp.clip(m0 - HALF_WIN, 0, n_sub - KBLK)
        bias = bias_ref[(m0 - ks) // HALF_WIN]
        qrow = pl.multiple_of(base + m0, QBLK)
        krow = pl.multiple_of(base + ks, HALF_WIN)
        for pi_ in range(2):
            cs = slice(128 * pi_, 128 * (pi_ + 1))
            q2 = q_ref[pl.ds(qrow, QBLK), cs]
            k2 = k_ref[pl.ds(krow, KBLK), cs]
            v2 = jnp.concatenate([v_ref[pl.ds(krow, KBLK), cs], ones], axis=1)
            qq = jnp.concatenate([jnp.where(head0, q2, zero), jnp.where(head0, zero, q2)], axis=0)
            s = lax.dot_general(qq, k2, (((1,), (1,)), ((), ())), preferred_element_type=F32)
            s = s + bias
            mx = jnp.max(s, axis=-1, keepdims=True)
            pv = _dot(jnp.exp(s - mx).astype(BF16), v2)
            rows = pl.ds(qrow, QBLK)
            o_ref[rows, cs] = jnp.where(low64, pv[:QBLK, :128], pv[QBLK:, :128]).astype(o_ref.dtype)
            m_ref[rows, cs] = jnp.where(low64, mx[:QBLK], mx[QBLK:])
            d_ref[rows, cs] = jnp.where(low64, pv[:QBLK, 128:], pv[QBLK:, 128:]).astype(d_ref.dtype)
        return carry

    lax.fori_loop(0, seq // QBLK, block, 0, unroll=16)


def _attn_kernel(q_ref, k_ref, v_ref, perm_ref, bias_ref, out_ref, qd_ref, kd_ref, vd_ref, od_ref, dd_ref, md_ref,
                 num_ref, mx_ref, den_ref, *, seq):
    g = pl.program_id(1)
    n_tiles = seq // PTILE

    def deinterleave(dil, pidx):
        n_loc = PTILE // dil
        n_sub = seq // dil
        pm = perm_ref[pidx]

        def tile(j, carry):
            rows = pl.ds(pl.multiple_of(j * PTILE, PTILE), PTILE)
            for src, dst in ((q_ref, qd_ref), (k_ref, kd_ref), (v_ref, vd_ref)):
                y = _dot(pm, src[rows, :]).astype(BF16)
                for r in range(dil):
                    dst[pl.ds(pl.multiple_of(r * n_sub + j * n_loc, n_loc), n_loc), :] = y[r * n_loc:(r + 1) * n_loc]
            return carry

        lax.fori_loop(0, n_tiles, tile, 0)

    def merge(dil, pidx, last):
        n_loc = PTILE // dil
        n_sub = seq // dil
        pm = perm_ref[pidx]

        def tile(j, carry):
            rows = pl.ds(pl.multiple_of(j * PTILE, PTILE), PTILE)

            def gathered(ref):
                return jnp.concatenate(
                    [ref[pl.ds(pl.multiple_of(r * n_sub + j * n_loc, n_loc), n_loc), :] for r in range(dil)], axis=0)

            o = _dot(pm, gathered(od_ref))
            d_new = _dot(pm, gathered(dd_ref))
            t = gathered(md_ref)
            hi = t.astype(BF16)
            lo = (t - hi.astype(F32)).astype(BF16)
            m_new = _dot(pm, hi) + _dot(pm, lo)
            m_old = mx_ref[rows, :]
            mx = jnp.maximum(m_old, m_new)
            a, b = jnp.exp(m_old - mx), jnp.exp(m_new - mx)
            num = num_ref[rows, :] * a + o * b
            den = den_ref[rows, :] * a + d_new * b
            if last:
                out_ref[rows, :] = (num / den).astype(BF16)
            else:
                num_ref[rows, :] = num
                mx_ref[rows, :] = mx
                den_ref[rows, :] = den
            return carry

        lax.fori_loop(0, n_tiles, tile, 0)

    n_groups = len(ATTN_PATTERNS)
    for gi, (_, dil) in enumerate(ATTN_PATTERNS):
        @pl.when(g == gi)
        def _():
            if gi == 0:
                _banded(q_ref, k_ref, v_ref, bias_ref, num_ref, mx_ref, den_ref, seq=seq, n_sub=seq)
            else:
                deinterleave(dil, 2 * (gi - 1))
                _banded(qd_ref, kd_ref, vd_ref, bias_ref, od_ref, md_ref, dd_ref, seq=seq, n_sub=seq // dil)
                merge(dil, 2 * (gi - 1) + 1, gi == n_groups - 1)


def _band_bias():
    r = np.arange(2 * QBLK)[:, None] & (QBLK - 1)
    j = np.arange(KBLK)[None, :]
    return jnp.asarray(np.stack([np.where(np.abs(j - HALF_WIN * i - r) <= HALF_WIN, 0.0, NEG_INF)
                                 for i in range(3)]), dtype=F32)


def _attn(q, k, v, perms, bias):
    bsz, seq, _ = q.shape
    w = ATTN_OUT_WIDTH
    blk = pl.BlockSpec((None, seq, w), lambda b, g: (b, 0, g))
    return pl.pallas_call(
        functools.partial(_attn_kernel, seq=seq),
        grid=(bsz, len(ATTN_PATTERNS)),
        in_specs=[blk, blk, blk,
                  pl.BlockSpec(perms.shape, lambda b, g: (0, 0, 0), pipeline_mode=pl.Buffered(1)),
                  pl.BlockSpec(bias.shape, lambda b, g: (0, 0, 0), pipeline_mode=pl.Buffered(1))],
        out_specs=pl.BlockSpec((None, seq, w), lambda b, g: (b, 0, 0)),
        out_shape=jax.ShapeDtypeStruct((bsz, seq, w), BF16),
        scratch_shapes=[pltpu.VMEM((seq, w), BF16)] * 5 + [pltpu.VMEM((seq, w), F32)] * 4,
        compiler_params=_params(2),
        name="attn",
    )(q, k, v, perms, bias)


def _interleave_perms():
    mats = []
    for _, dil in ATTN_PATTERNS[1:]:
        n_loc = PTILE // dil
        p = np.zeros((PTILE, PTILE), np.float32)
        pos = np.arange(PTILE)
        p[(pos % dil) * n_loc + pos // dil, pos] = 1.0
        mats.extend([p, p.T])
    return jnp.asarray(np.stack(mats), dtype=BF16)


def _mix_kernel(x_ref, z_ref, a_ref, perm_ref, wg_ref, bg_ref, wglu_ref, wbr_ref, wout_ref, g_ref, b_ref, h_ref,
                *, bsz, ts):
    tm = bsz * ts
    x = x_ref[...].reshape(tm, D_MODEL)
    xb = x.astype(BF16)
    gates = jax.nn.sigmoid(_dot(xb, wg_ref[...]) + bg_ref[...])
    zs = [[z_ref[p, :, 128 * j:128 * (j + 1)].astype(F32) for j in range(4)] for p in range(N_PAIRS)]
    z = _dot(perm_ref[...], _unplace(zs, tm // CHUNK).astype(BF16)).astype(BF16)
    gv = _dot(z, wglu_ref[...])
    ssm_out = gv[:, :D_MODEL] * jax.nn.sigmoid(gv[:, D_MODEL:])
    attn_out = _dot(a_ref[...].reshape(tm, ATTN_OUT_WIDTH), wbr_ref[...])
    mix = gates[:, :D_MODEL] * ssm_out + gates[:, D_MODEL:] * attn_out
    mixed = _dot(mix.astype(BF16), wout_ref[...])
    h = _layer_norm(DEEPNORM_ALPHA * x + mixed, g_ref[...], b_ref[...])
    h_ref[...] = h.reshape(bsz, ts, D_MODEL)


def _mix(x, z, attn, perm_t, wg, bg, wglu, wbr, wout, g, b, ts):
    bsz, seq, _ = x.shape
    tm = bsz * ts
    tile = lambda w_: pl.BlockSpec((bsz, ts, w_), lambda i: (0, i, 0))
    wspec = lambda shape: pl.BlockSpec(shape, lambda i: (0, 0), pipeline_mode=pl.Buffered(1))
    return pl.pallas_call(
        functools.partial(_mix_kernel, bsz=bsz, ts=ts),
        grid=(seq // ts,),
        in_specs=[tile(D_MODEL), pl.BlockSpec((N_PAIRS, tm // CHUNK, PAIR_W), lambda i: (0, i, 0)),
                  tile(ATTN_OUT_WIDTH), wspec((tm, tm)),
                  wspec((D_MODEL, 2 * D_MODEL)), wspec((1, 2 * D_MODEL)), wspec((SSM_WIDTH, 2 * D_MODEL)),
                  wspec((ATTN_OUT_WIDTH, D_MODEL)), wspec((D_MODEL, D_MODEL)), wspec((1, D_MODEL)), wspec((1, D_MODEL))],
        out_specs=tile(D_MODEL),
        out_shape=jax.ShapeDtypeStruct((bsz, seq, D_MODEL), F32),
        compiler_params=_params(1),
        name="mix",
    )(x, z, attn, perm_t, wg, bg, wglu, wbr, wout, g, b)


FFN_CHUNK = 256
FFN_TM = 1024
FFN_HALO = 16


def _ffn_kernel(h_ref, hp_ref, hn_ref, wup_ref, cw_ref, cb_ref, wdn_ref, g_ref, b_ref, o_ref, act_ref,
                *, tm, tiles_per_seq):
    i = pl.program_id(0)
    j = i % tiles_per_seq
    h = h_ref[...]
    hb = h.astype(BF16)
    prev_ok = (j > 0).astype(F32)
    next_ok = (j < tiles_per_seq - 1).astype(F32)
    hpb = (hp_ref[...] * prev_ok).astype(BF16)
    hnb = (hn_ref[...] * next_ok).astype(BF16)
    hext = jnp.concatenate([hpb, hb, hnb], axis=0)
    rows = tm + 2 * FFN_HALO
    mid = slice(FFN_HALO, FFN_HALO + tm)
    for c in range(D_FF // FFN_CHUNK):
        halves = []
        for c0 in (c * FFN_CHUNK, D_FF + c * FFN_CHUNK):
            u = _dot(hext, wup_ref[:, c0:c0 + FFN_CHUNK])
            cw = cw_ref[:, c0:c0 + FFN_CHUNK]
            before = pltpu.roll(u, 1, 0)[mid]
            after = pltpu.roll(u, rows - 1, 0)[mid]
            halves.append(before * cw[0:1] + u[mid] * cw[1:2] + after * cw[2:3] + cb_ref[:, c0:c0 + FFN_CHUNK])
        a, val = halves
        act_ref[:, c * FFN_CHUNK:(c + 1) * FFN_CHUNK] = (_gelu(a) * val).astype(BF16)
    ffn = _dot(act_ref[...], wdn_ref[...])
    o_ref[...] = _layer_norm(DEEPNORM_ALPHA * h + ffn, g_ref[...], b_ref[...])


def _ffn(h, wup, cw, cb, wdn, g, b, tm, seq):
    t = h.shape[0]
    tps = seq // tm
    nblk = t // FFN_HALO
    fixed = lambda i: (0, 0)
    wspec = lambda shape: pl.BlockSpec(shape, fixed, pipeline_mode=pl.Buffered(1))
    return pl.pallas_call(
        functools.partial(_ffn_kernel, tm=tm, tiles_per_seq=tps),
        grid=(t // tm,),
        in_specs=[pl.BlockSpec((tm, D_MODEL), lambda i: (i, 0)),
                  pl.BlockSpec((FFN_HALO, D_MODEL), lambda i: (jnp.maximum(i * (tm // FFN_HALO) - 1, 0), 0)),
                  pl.BlockSpec((FFN_HALO, D_MODEL), lambda i: (jnp.minimum((i + 1) * (tm // FFN_HALO), nblk - 1), 0)),
                  wspec((D_MODEL, 2 * D_FF)), wspec((8, 2 * D_FF)), wspec((1, 2 * D_FF)),
                  wspec((D_FF, D_MODEL)), wspec((1, D_MODEL)), wspec((1, D_MODEL))],
        out_specs=pl.BlockSpec((tm, D_MODEL), lambda i: (i, 0)),
        out_shape=jax.ShapeDtypeStruct((t, D_MODEL), F32),
        scratch_shapes=[pltpu.VMEM((tm, D_FF), BF16)],
        compiler_params=_params(1),
        name="ffn",
    )(h, h, h, wup, cw, cb, wdn, g, b)


def _step_major_perm(bsz, ts):
    p = np.zeros((bsz * ts, bsz * ts), np.float32)
    b, k, s = np.meshgrid(np.arange(bsz), np.arange(ts // CHUNK), np.arange(CHUNK), indexing="ij")
    p[(s * (ts // CHUNK) + k) * bsz + b, (b * (ts // CHUNK) + k) * CHUNK + s] = 1.0
    return p


def _layer(x, positions, w_in, b_in, lam_re, lam_im, log_dt, b_re, b_im, c_re, c_im, d_skip,
           w_glu_v, w_glu_g, w_attn_br, w_out, ln1_g, ln1_b, w_up, conv_w, conv_b, w_down, ln2_g, ln2_b):
    bsz, seq, _ = x.shape
    t = bsz * seq
    ts = 32
    n_chunks = seq // CHUNK

    q0, k0, v0, g0 = SSM_WIDTH, SSM_WIDTH + ATTN_WIDTH, SSM_WIDTH + 2 * ATTN_WIDTH, SSM_WIDTH + 3 * ATTN_WIDTH
    col_scale = np.ones((g0,), np.float32)
    col_scale[q0:k0] = HEAD_DIM ** -0.5
    w_proj = (w_in[:, :g0] * col_scale).astype(BF16)
    b_proj = (b_in[:g0] * col_scale)[None, :]
    inv_freq = jnp.power(ROPE_THETA, -jnp.arange(32, dtype=F32) * 2.0 / HEAD_DIM)
    invf = jnp.tile(inv_freq, 4)[None, :]
    sgn = jnp.asarray(np.tile(np.repeat(np.float32([-1.0, 1.0]), 32), 2))[None, :]
    step_perm = _step_major_perm(bsz, ts)

    vp, q, k, v = _proj(x, positions[:, :, None], w_proj, b_proj, invf, sgn,
                        jnp.asarray(step_perm, dtype=BF16), ts)

    m_pair, q_pair, p_pair, dec = _ssm_params(lam_re, lam_im, log_dt, b_re, b_im, c_re, c_im, d_skip)
    z = _ssm(vp, m_pair, q_pair, p_pair, dec, n_chunks, bsz)

    attn = _attn(q, k, v, _interleave_perms(), _band_bias())

    wg = w_in[:, g0:].astype(BF16)
    bg = b_in[g0:][None, :]
    wglu = jnp.concatenate([w_glu_v, w_glu_g], axis=1).astype(BF16)
    h = _mix(x, z, attn, jnp.asarray(step_perm.T, dtype=BF16), wg, bg, wglu, w_attn_br.astype(BF16),
             w_out.astype(BF16), ln1_g[None, :], ln1_b[None, :], ts)

    cw = jnp.concatenate([conv_w, jnp.zeros((5, 2 * D_FF), F32)], axis=0)
    out = _ffn(h.reshape(t, D_MODEL), w_up.astype(BF16), cw, conv_b[None, :], w_down.astype(BF16),
               ln2_g[None, :], ln2_b[None, :], FFN_TM, seq)
    return out.reshape(bsz, seq, D_MODEL)


def kernel(x, positions, w_in, b_in, ssm_lam_re, ssm_lam_im, ssm_log_dt, ssm_b_re, ssm_b_im, ssm_c_re, ssm_c_im, ssm_d, w_glu_v, w_glu_g, w_attn_br, w_out, ln1_g, ln1_b, w_up, conv_w, conv_b, w_down, ln2_g, ln2_b):
    h = x
    for layer in range(w_in.shape[0]):
        h = _layer(h, positions, w_in[layer], b_in[layer], ssm_lam_re[layer], ssm_lam_im[layer],
                   ssm_log_dt[layer], ssm_b_re[layer], ssm_b_im[layer], ssm_c_re[layer], ssm_c_im[layer],
                   ssm_d[layer], w_glu_v[layer], w_glu_g[layer], w_attn_br[layer], w_out[layer],
                   ln1_g[layer], ln1_b[layer], w_up[layer], conv_w[layer], conv_b[layer], w_down[layer],
                   ln2_g[layer], ln2_b[layer])
    return h
```

```python
import functools
import math

import numpy as np
import jax
import jax.numpy as jnp
from jax import lax
from jax.experimental import pallas as pl
from jax.experimental.pallas import tpu as pltpu

D_MODEL = 1024
SSM_WIDTH = 512
SSM_GROUP = 16
SSM_GROUPS = 32
SSM_STATE = 64
HEAD_DIM = 64
HEADS_PER_GROUP = 4
ATTN_PATTERNS = ((128, 1), (512, 4), (2048, 16))
N_HEADS = 12
ATTN_WIDTH = 768
ATTN_OUT_WIDTH = 256
D_FF = 2816
LN_EPS = 1e-5
NEG_INF = -1e30
ROPE_THETA = 10000.0
DEEPNORM_ALPHA = 2.0 ** 0.25

CHUNK = 16
PAIR_W = 2 * SSM_GROUP * CHUNK
N_PAIRS = SSM_GROUPS // 2
HALF_WIN = 64
QBLK = 128
KBLK = 256
PTILE = 256
PROJ_W = SSM_WIDTH + 3 * ATTN_WIDTH
VMEM_LIMIT = 56 * 1024 * 1024
F32 = jnp.float32
BF16 = jnp.bfloat16


def _dot(a, b):
    return jnp.dot(a, b, preferred_element_type=F32)


def _gelu(x):
    return 0.5 * x * (1.0 + lax.erf(x * np.float32(math.sqrt(0.5))))


def _layer_norm(r, g, b):
    mu = jnp.mean(r, axis=-1, keepdims=True)
    c = r - mu
    var = jnp.mean(c * c, axis=-1, keepdims=True)
    return c * lax.rsqrt(var + LN_EPS) * g + b


def _params(n_axes):
    return pltpu.CompilerParams(dimension_semantics=("arbitrary",) * n_axes, vmem_limit_bytes=VMEM_LIMIT)


def _place(r, n_rows):
    lane_grp = lax.broadcasted_iota(jnp.int32, (n_rows, 128), 1) // 32
    out = [[None] * 4 for _ in range(N_PAIRS)]
    for j in range(4):
        for q in range(4):
            src = [r[(4 * j + m) * n_rows:(4 * j + m + 1) * n_rows, 128 * q:128 * (q + 1)] for m in range(4)]
            rolled = [[s if sh == 0 else pltpu.roll(s, 32 * sh, 1) for sh in range(4)] for s in src]
            for pp in range(4):
                d = rolled[0][(0 - pp) % 4]
                for m in range(1, 4):
                    d = jnp.where(lane_grp == m, rolled[m][(m - pp) % 4], d)
                out[4 * q + pp][j] = d
    return out


def _unplace(zs, n_rows):
    lane_grp = lax.broadcasted_iota(jnp.int32, (n_rows, 128), 1) // 32
    rows = []
    for j in range(4):
        for m in range(4):
            cols = []
            for q in range(4):
                d = None
                for pp in range(4):
                    s = zs[4 * q + pp][j]
                    sh = (pp - m) % 4
                    rl = s if sh == 0 else pltpu.roll(s, 32 * sh, 1)
                    d = rl if d is None else jnp.where(lane_grp == pp, rl, d)
                cols.append(d)
            rows.append(jnp.concatenate(cols, axis=1))
    return jnp.concatenate(rows, axis=0)


def _proj_kernel(x_ref, pos_ref, w_ref, b_ref, invf_ref, sgn_ref, perm_ref, vp_ref, q_ref, k_ref, v_ref, *, bsz, ts):
    tm = bsz * ts
    xb = x_ref[...].reshape(tm, D_MODEL).astype(BF16)
    posf = jnp.broadcast_to(pos_ref[...].astype(F32)[:, None, :], (bsz, ts, ts)).reshape(tm, ts)
    own = (lax.broadcasted_iota(jnp.int32, (tm, ts), 0) & (ts - 1)) == lax.broadcasted_iota(jnp.int32, (tm, ts), 1)
    pos_col = jnp.sum(jnp.where(own, posf, 0.0), axis=1, keepdims=True)
    ang = pos_col * invf_ref[...]
    cos = jnp.cos(ang)
    sin = jnp.sin(ang) * sgn_ref[...]
    low_half = (lax.broadcasted_iota(jnp.int32, (tm, 128), 1) & 32) == 0
    u = _dot(xb, w_ref[:, 0:SSM_WIDTH]) + b_ref[:, 0:SSM_WIDTH]
    r = _dot(perm_ref[...], u.astype(BF16))
    placed = _place(r, tm // CHUNK)
    for p in range(N_PAIRS):
        for j in range(4):
            vp_ref[p, :, 128 * j:128 * (j + 1)] = placed[p][j].astype(BF16)
    for dst, c0 in ((q_ref, SSM_WIDTH), (k_ref, SSM_WIDTH + ATTN_WIDTH)):
        t = _dot(xb, w_ref[:, c0:c0 + ATTN_WIDTH]) + b_ref[:, c0:c0 + ATTN_WIDTH]
        for j in range(ATTN_WIDTH // 128):
            tj = t[:, 128 * j:128 * (j + 1)]
            partner = jnp.where(low_half, pltpu.roll(tj, 96, 1), pltpu.roll(tj, 32, 1))
            dst[:, :, 128 * j:128 * (j + 1)] = (tj * cos + partner * sin).astype(BF16).reshape(bsz, ts, 128)
    c0 = SSM_WIDTH + 2 * ATTN_WIDTH
    v = _dot(xb, w_ref[:, c0:c0 + ATTN_WIDTH]) + b_ref[:, c0:c0 + ATTN_WIDTH]
    v_ref[...] = v.astype(BF16).reshape(bsz, ts, ATTN_WIDTH)


def _proj(x, pos, w, b, invf, sgn, perm, ts):
    bsz, seq, _ = x.shape
    tm = bsz * ts
    rows = tm // CHUNK
    tile = lambda w_: pl.BlockSpec((bsz, ts, w_), lambda i: (0, i, 0))
    fixed = lambda i: (0, 0)
    return pl.pallas_call(
        functools.partial(_proj_kernel, bsz=bsz, ts=ts),
        grid=(seq // ts,),
        in_specs=[tile(D_MODEL), pl.BlockSpec((None, bsz, ts), lambda i: (i, 0, 0)),
                  pl.BlockSpec((D_MODEL, PROJ_W), fixed, pipeline_mode=pl.Buffered(1)),
                  pl.BlockSpec((1, PROJ_W), fixed),
                  pl.BlockSpec((1, 128), fixed),
                  pl.BlockSpec((1, 128), fixed),
                  pl.BlockSpec((tm, tm), fixed, pipeline_mode=pl.Buffered(1))],
        out_specs=[pl.BlockSpec((N_PAIRS, rows, PAIR_W), lambda i: (0, i, 0)),
                   tile(ATTN_WIDTH), tile(ATTN_WIDTH), tile(ATTN_WIDTH)],
        out_shape=[jax.ShapeDtypeStruct((N_PAIRS, seq // CHUNK * bsz, PAIR_W), BF16),
                   jax.ShapeDtypeStruct((bsz, seq, ATTN_WIDTH), BF16),
                   jax.ShapeDtypeStruct((bsz, seq, ATTN_WIDTH), BF16),
                   jax.ShapeDtypeStruct((bsz, seq, ATTN_WIDTH), BF16)],
        compiler_params=_params(1),
        name="proj",
    )(x, pos, w, b, invf, sgn, perm)


def _ssm_kernel(v_ref, m_ref, q_ref, p_ref, a_ref, z_ref, h_ref, *, n_chunks, bsz):
    v = v_ref[...]
    h_ref[...] = _dot(v, q_ref[...])
    dec = a_ref[...]
    afr, afi, abr, abi = (jnp.broadcast_to(dec[i:i + 1, :], (bsz, 128)) for i in range(4))

    def step(k, carry):
        hfr, hfi, hbr, hbi = carry
        rf = pl.ds(pl.multiple_of(k * bsz, bsz), bsz)
        rb = pl.ds(pl.multiple_of((n_chunks - 1 - k) * bsz, bsz), bsz)
        xfr = h_ref[rf, 0:128]
        xfi = h_ref[rf, 128:256]
        xbr = h_ref[rb, 256:384]
        xbi = h_ref[rb, 384:512]
        h_ref[rf, 0:128] = hfr
        h_ref[rf, 128:256] = hfi
        h_ref[rb, 256:384] = hbr
        h_ref[rb, 384:512] = hbi
        return (afr * hfr - afi * hfi + xfr, afr * hfi + afi * hfr + xfi,
                abr * hbr - abi * hbi + xbr, abr * hbi + abi * hbr + xbi)

    zero = jnp.zeros((bsz, 128), F32)
    lax.fori_loop(0, n_chunks, step, (zero, zero, zero, zero), unroll=4)
    y = _dot(v, m_ref[...]) + _dot(h_ref[...].astype(BF16), p_ref[...])
    z_ref[...] = _gelu(y).astype(BF16)


def _ssm(vp, m, q, p, a, n_chunks, bsz):
    rows = n_chunks * bsz
    blk = lambda shape: pl.BlockSpec((None,) + shape, lambda i: (i, 0, 0))
    return pl.pallas_call(
        functools.partial(_ssm_kernel, n_chunks=n_chunks, bsz=bsz),
        grid=(N_PAIRS,),
        in_specs=[blk((rows, PAIR_W)), blk((PAIR_W, PAIR_W)), blk((PAIR_W, PAIR_W)),
                  blk((PAIR_W, PAIR_W)), blk((8, 128))],
        out_specs=blk((rows, PAIR_W)),
        out_shape=jax.ShapeDtypeStruct((N_PAIRS, rows, PAIR_W), BF16),
        scratch_shapes=[pltpu.VMEM((rows, PAIR_W), F32)],
        compiler_params=_params(1),
        name="ssm",
    )(vp, m, q, p, a)


def _ssm_params(lam_re, lam_im, log_dt, b_re, b_im, c_re, c_im, d_skip):
    hp = lax.Precision.HIGHEST
    dt = jnp.exp(log_dt)[..., None]
    xr, xi = lam_re * dt, lam_im * dt
    abar_m1_r = jnp.expm1(xr) * jnp.cos(xi) - 2.0 * jnp.square(jnp.sin(0.5 * xi))
    abar_i = jnp.exp(xr) * jnp.sin(xi)
    den = lam_re * lam_re + lam_im * lam_im
    kr = (abar_m1_r * lam_re + abar_i * lam_im) / den
    ki = (abar_i * lam_re - abar_m1_r * lam_im) / den
    bbr = kr[..., None] * b_re - ki[..., None] * b_im
    bbi = kr[..., None] * b_im + ki[..., None] * b_re
    n = jnp.arange(CHUNK + 1, dtype=F32)[:, None, None, None]
    mag = jnp.exp(n * xr[None])
    pr, pi = mag * jnp.cos(n * xi[None]), mag * jnp.sin(n * xi[None])
    wr = c_re[None] * pr[:, :, :, None, :] - c_im[None] * pi[:, :, :, None, :]
    wi = c_re[None] * pi[:, :, :, None, :] + c_im[None] * pr[:, :, :, None, :]
    kern = jnp.einsum('ndgop,dgpc->ndgoc', jnp.concatenate([wr, -wi], axis=-1),
                      jnp.concatenate([bbr, bbi], axis=-2), precision=hp)
    skip = jnp.eye(SSM_GROUP, dtype=F32)[None] * d_skip.reshape(SSM_GROUPS, SSM_GROUP)[:, :, None]
    lags = jnp.concatenate([kern[CHUNK - 1:0:-1, 1], (kern[0, 0] + kern[0, 1] + skip)[None], kern[1:CHUNK, 0]], axis=0)
    toep = jnp.stack([lags[CHUNK - 1 - s:2 * CHUNK - 1 - s] for s in range(CHUNK)], axis=0)

    def in_map(d, ar, ai):
        qr = ar[..., None] * bbr[d][None] - ai[..., None] * bbi[d][None]
        qi = ar[..., None] * bbi[d][None] + ai[..., None] * bbr[d][None]
        return [qr, qi]

    q_parts = in_map(0, pr[CHUNK - 1::-1, 0], pi[CHUNK - 1::-1, 0]) + in_map(1, pr[:CHUNK, 1], pi[:CHUNK, 1])
    p_parts = [wr[1:CHUNK + 1, 0], -wi[1:CHUNK + 1, 0], wr[CHUNK:0:-1, 1], -wi[CHUNK:0:-1, 1]]
    eye2 = jnp.eye(2, dtype=BF16)[None, None, :, None, None, :, None]
    pair = lambda z: (z[:, :, :, :, :, None, :] * eye2).reshape(N_PAIRS, PAIR_W, PAIR_W)
    tb = toep.astype(BF16).reshape(CHUNK, CHUNK, N_PAIRS, 2, SSM_GROUP, SSM_GROUP)
    m_pair = pair(jnp.transpose(tb, (2, 0, 3, 5, 1, 4)))
    qs = jnp.stack(q_parts, axis=0).astype(BF16).reshape(4, CHUNK, N_PAIRS, 2, SSM_STATE, SSM_GROUP)
    q_pair = pair(jnp.transpose(qs, (2, 1, 3, 5, 0, 4)))
    ps = jnp.stack(p_parts, axis=0).astype(BF16).reshape(4, CHUNK, N_PAIRS, 2, SSM_GROUP, SSM_STATE)
    p_pair = pair(jnp.transpose(ps, (2, 0, 3, 5, 1, 4)))
    dec = jnp.stack([pr[CHUNK, 0], pi[CHUNK, 0], pr[CHUNK, 1], pi[CHUNK, 1]], axis=0)
    dec = jnp.transpose(dec.reshape(4, N_PAIRS, 128), (1, 0, 2))
    dec = jnp.concatenate([dec, jnp.zeros((N_PAIRS, 4, 128), F32)], axis=1)
    return m_pair, q_pair, p_pair, dec


def _banded(q_ref, k_ref, v_ref, bias_ref, o_ref, m_ref, d_ref, *, seq, n_sub):
    lane = lax.broadcasted_iota(jnp.int32, (QBLK, 128), 1)
    low64 = lane < 64
    head0 = low64
    zero = jnp.zeros((QBLK, 128), BF16)
    ones = jnp.ones((KBLK, 128), BF16)
    blocks_per_seq = n_sub // QBLK

    def block(i, carry):
        base = (i // blocks_per_seq) * n_sub
        m0 = (i % blocks_per_seq) * QBLK
        ks = jnp.clip(m0 - HALF_WIN, 0, n_sub - KBLK)
        bias = bias_ref[(m0 - ks) // HALF_WIN]
        qrow = pl.multiple_of(base + m0, QBLK)
        krow = pl.multiple_of(base + ks, HALF_WIN)
        for pi_ in range(2):
            cs = slice(128 * pi_, 128 * (pi_ + 1))
            q2 = q_ref[pl.ds(qrow, QBLK), cs]
            k2 = k_ref[pl.ds(krow, KBLK), cs]
            v2 = jnp.concatenate([v_ref[pl.ds(krow, KBLK), cs], ones], axis=1)
            qq = jnp.concatenate([jnp.where(head0, q2, zero), jnp.where(head0, zero, q2)], axis=0)
            s = lax.dot_general(qq, k2, (((1,), (1,)), ((), ())), preferred_element_type=F32)
            s = s + bias
            mx = jnp.max(s, axis=-1, keepdims=True)
            pv = _dot(jnp.exp(s - mx).astype(BF16), v2)
            rows = pl.ds(qrow, QBLK)
            o_ref[rows, cs] = jnp.where(low64, pv[:QBLK, :128], pv[QBLK:, :128]).astype(o_ref.dtype)
            m_ref[rows, cs] = jnp.where(low64, mx[:QBLK], mx[QBLK:])
            d_ref[rows, cs] = jnp.where(low64, pv[:QBLK, 128:], pv[QBLK:, 128:]).astype(d_ref.dtype)
        return carry

    lax.fori_loop(0, seq // QBLK, block, 0, unroll=16)


def _attn_kernel(q_ref, k_ref, v_ref, perm_ref, bias_ref, out_ref, qd_ref, kd_ref, vd_ref, od_ref, dd_ref, md_ref,
                 num_ref, mx_ref, den_ref, *, seq):
    g = pl.program_id(1)
    n_tiles = seq // PTILE

    def deinterleave(dil, pidx):
        n_loc = PTILE // dil
        n_sub = seq // dil
        pm = perm_ref[pidx]

        def tile(j, carry):
            rows = pl.ds(pl.multiple_of(j * PTILE, PTILE), PTILE)
            for src, dst in ((q_ref, qd_ref), (k_ref, kd_ref), (v_ref, vd_ref)):
                y = _dot(pm, src[rows, :]).astype(BF16)
                for r in range(dil):
                    dst[pl.ds(pl.multiple_of(r * n_sub + j * n_loc, n_loc), n_loc), :] = y[r * n_loc:(r + 1) * n_loc]
            return carry

        lax.fori_loop(0, n_tiles, tile, 0)

    def merge(dil, pidx, last):
        n_loc = PTILE // dil
        n_sub = seq // dil
        pm = perm_ref[pidx]

        def tile(j, carry):
            rows = pl.ds(pl.multiple_of(j * PTILE, PTILE), PTILE)

            def gathered(ref):
                return jnp.concatenate(
                    [ref[pl.ds(pl.multiple_of(r * n_sub + j * n_loc, n_loc), n_loc), :] for r in range(dil)], axis=0)

            o = _dot(pm, gathered(od_ref))
            d_new = _dot(pm, gathered(dd_ref))
            t = gathered(md_ref)
            hi = t.astype(BF16)
            lo = (t - hi.astype(F32)).astype(BF16)
            m_new = _dot(pm, hi) + _dot(pm, lo)
            m_old = mx_ref[rows, :]
            mx = jnp.maximum(m_old, m_new)
            a, b = jnp.exp(m_old - mx), jnp.exp(m_new - mx)
            num = num_ref[rows, :] * a + o * b
            den = den_ref[rows, :] * a + d_new * b
            if last:
                out_ref[rows, :] = (num / den).astype(BF16)
            else:
                num_ref[rows, :] = num
                mx_ref[rows, :] = mx
                den_ref[rows, :] = den
            return carry

        lax.fori_loop(0, n_tiles, tile, 0)

    n_groups = len(ATTN_PATTERNS)
    for gi, (_, dil) in enumerate(ATTN_PATTERNS):
        @pl.when(g == gi)
        def _():
            if gi == 0:
                _banded(q_ref, k_ref, v_ref, bias_ref, num_ref, mx_ref, den_ref, seq=seq, n_sub=seq)
            else:
                deinterleave(dil, 2 * (gi - 1))
                _banded(qd_ref, kd_ref, vd_ref, bias_ref, od_ref, md_ref, dd_ref, seq=seq, n_sub=seq // dil)
                merge(dil, 2 * (gi - 1) + 1, gi == n_groups - 1)


def _band_bias():
    r = np.arange(2 * QBLK)[:, None] & (QBLK - 1)
    j = np.arange(KBLK)[None, :]
    return jnp.asarray(np.stack([np.where(np.abs(j - HALF_WIN * i - r) <= HALF_WIN, 0.0, NEG_INF)
                                 for i in range(3)]), dtype=F32)


def _attn(q, k, v, perms, bias):
    bsz, seq, _ = q.shape
    w = ATTN_OUT_WIDTH
    blk = pl.BlockSpec((None, seq, w), lambda b, g: (b, 0, g))
    return pl.pallas_call(
        functools.partial(_attn_kernel, seq=seq),
        grid=(bsz, len(ATTN_PATTERNS)),
        in_specs=[blk, blk, blk,
                  pl.BlockSpec(perms.shape, lambda b, g: (0, 0, 0), pipeline_mode=pl.Buffered(1)),
                  pl.BlockSpec(bias.shape, lambda b, g: (0, 0, 0), pipeline_mode=pl.Buffered(1))],
        out_specs=pl.BlockSpec((None, seq, w), lambda b, g: (b, 0, 0)),
        out_shape=jax.ShapeDtypeStruct((bsz, seq, w), BF16),
        scratch_shapes=[pltpu.VMEM((seq, w), BF16)] * 5 + [pltpu.VMEM((seq, w), F32)] * 4,
        compiler_params=_params(2),
        name="attn",
    )(q, k, v, perms, bias)


def _interleave_perms():
    mats = []
    for _, dil in ATTN_PATTERNS[1:]:
        n_loc = PTILE // dil
        p = np.zeros((PTILE, PTILE), np.float32)
        pos = np.arange(PTILE)
        p[(pos % dil) * n_loc + pos // dil, pos] = 1.0
        mats.extend([p, p.T])
    return jnp.asarray(np.stack(mats), dtype=BF16)


def _mix_kernel(x_ref, z_ref, a_ref, perm_ref, wg_ref, bg_ref, wglu_ref, wbr_ref, wout_ref, g_ref, b_ref, h_ref,
                *, bsz, ts):
    tm = bsz * ts
    x = x_ref[...].reshape(tm, D_MODEL)
    xb = x.astype(BF16)
    gates = jax.nn.sigmoid(_dot(xb, wg_ref[...]) + bg_ref[...])
    zs = [[z_ref[p, :, 128 * j:128 * (j + 1)].astype(F32) for j in range(4)] for p in range(N_PAIRS)]
    z = _dot(perm_ref[...], _unplace(zs, tm // CHUNK).astype(BF16)).astype(BF16)
    gv = _dot(z, wglu_ref[...])
    ssm_out = gv[:, :D_MODEL] * jax.nn.sigmoid(gv[:, D_MODEL:])
    attn_out = _dot(a_ref[...].reshape(tm, ATTN_OUT_WIDTH), wbr_ref[...])
    mix = gates[:, :D_MODEL] * ssm_out + gates[:, D_MODEL:] * attn_out
    mixed = _dot(mix.astype(BF16), wout_ref[...])
    h = _layer_norm(DEEPNORM_ALPHA * x + mixed, g_ref[...], b_ref[...])
    h_ref[...] = h.reshape(bsz, ts, D_MODEL)


def _mix(x, z, attn, perm_t, wg, bg, wglu, wbr, wout, g, b, ts):
    bsz, seq, _ = x.shape
    tm = bsz * ts
    tile = lambda w_: pl.BlockSpec((bsz, ts, w_), lambda i: (0, i, 0))
    wspec = lambda shape: pl.BlockSpec(shape, lambda i: (0, 0), pipeline_mode=pl.Buffered(1))
    return pl.pallas_call(
        functools.partial(_mix_kernel, bsz=bsz, ts=ts),
        grid=(seq // ts,),
        in_specs=[tile(D_MODEL), pl.BlockSpec((N_PAIRS, tm // CHUNK, PAIR_W), lambda i: (0, i, 0)),
                  tile(ATTN_OUT_WIDTH), wspec((tm, tm)),
                  wspec((D_MODEL, 2 * D_MODEL)), wspec((1, 2 * D_MODEL)), wspec((SSM_WIDTH, 2 * D_MODEL)),
                  wspec((ATTN_OUT_WIDTH, D_MODEL)), wspec((D_MODEL, D_MODEL)), wspec((1, D_MODEL)), wspec((1, D_MODEL))],
        out_specs=tile(D_MODEL),
        out_shape=jax.ShapeDtypeStruct((bsz, seq, D_MODEL), F32),
        compiler_params=_params(1),
        name="mix",
    )(x, z, attn, perm_t, wg, bg, wglu, wbr, wout, g, b)


FFN_CHUNK = 256
FFN_TM = 1024
FFN_HALO = 16


def _ffn_kernel(h_ref, hp_ref, hn_ref, wup_ref, cw_ref, cb_ref, wdn_ref, g_ref, b_ref, o_ref, act_ref,
                *, tm, tiles_per_seq):
    i = pl.program_id(0)
    j = i % tiles_per_seq
    h = h_ref[...]
    hb = h.astype(BF16)
    prev_ok = (j > 0).astype(F32)
    next_ok = (j < tiles_per_seq - 1).astype(F32)
    hpb = (hp_ref[...] * prev_ok).astype(BF16)
    hnb = (hn_ref[...] * next_ok).astype(BF16)
    hext = jnp.concatenate([hpb, hb, hnb], axis=0)
    rows = tm + 2 * FFN_HALO
    mid = slice(FFN_HALO, FFN_HALO + tm)
    for c in range(D_FF // FFN_CHUNK):
        halves = []
        for c0 in (c * FFN_CHUNK, D_FF + c * FFN_CHUNK):
            u = _dot(hext, wup_ref[:, c0:c0 + FFN_CHUNK])
            cw = cw_ref[:, c0:c0 + FFN_CHUNK]
            before = pltpu.roll(u, 1, 0)[mid]
            after = pltpu.roll(u, rows - 1, 0)[mid]
            halves.append(before * cw[0:1] + u[mid] * cw[1:2] + after * cw[2:3] + cb_ref[:, c0:c0 + FFN_CHUNK])
        a, val = halves
        act_ref[:, c * FFN_CHUNK:(c + 1) * FFN_CHUNK] = (_gelu(a) * val).astype(BF16)
    ffn = _dot(act_ref[...], wdn_ref[...])
    o_ref[...] = _layer_norm(DEEPNORM_ALPHA * h + ffn, g_ref[...], b_ref[...])


def _ffn(h, wup, cw, cb, wdn, g, b, tm, seq):
    t = h.shape[0]
    tps = seq // tm
    nblk = t // FFN_HALO
    fixed = lambda i: (0, 0)
    wspec = lambda shape: pl.BlockSpec(shape, fixed, pipeline_mode=pl.Buffered(1))
    return pl.pallas_call(
        functools.partial(_ffn_kernel, tm=tm, tiles_per_seq=tps),
        grid=(t // tm,),
        in_specs=[pl.BlockSpec((tm, D_MODEL), lambda i: (i, 0)),
                  pl.BlockSpec((FFN_HALO, D_MODEL), lambda i: (jnp.maximum(i * (tm // FFN_HALO) - 1, 0), 0)),
                  pl.BlockSpec((FFN_HALO, D_MODEL), lambda i: (jnp.minimum((i + 1) * (tm // FFN_HALO), nblk - 1), 0)),
                  wspec((D_MODEL, 2 * D_FF)), wspec((8, 2 * D_FF)), wspec((1, 2 * D_FF)),
                  wspec((D_FF, D_MODEL)), wspec((1, D_MODEL)), wspec((1, D_MODEL))],
        out_specs=pl.BlockSpec((tm, D_MODEL), lambda i: (i, 0)),
        out_shape=jax.ShapeDtypeStruct((t, D_MODEL), F32),
        scratch_shapes=[pltpu.VMEM((tm, D_FF), BF16)],
        compiler_params=_params(1),
        name="ffn",
    )(h, h, h, wup, cw, cb, wdn, g, b)


def _step_major_perm(bsz, ts):
    p = np.zeros((bsz * ts, bsz * ts), np.float32)
    b, k, s = np.meshgrid(np.arange(bsz), np.arange(ts // CHUNK), np.arange(CHUNK), indexing="ij")
    p[(s * (ts // CHUNK) + k) * bsz + b, (b * (ts // CHUNK) + k) * CHUNK + s] = 1.0
    return p


def _layer(x, positions, w_in, b_in, lam_re, lam_im, log_dt, b_re, b_im, c_re, c_im, d_skip,
           w_glu_v, w_glu_g, w_attn_br, w_out, ln1_g, ln1_b, w_up, conv_w, conv_b, w_down, ln2_g, ln2_b):
    bsz, seq, _ = x.shape
    t = bsz * seq
    ts = 32
    n_chunks = seq // CHUNK

    q0, k0, v0, g0 = SSM_WIDTH, SSM_WIDTH + ATTN_WIDTH, SSM_WIDTH + 2 * ATTN_WIDTH, SSM_WIDTH + 3 * ATTN_WIDTH
    col_scale = np.ones((g0,), np.float32)
    col_scale[q0:k0] = HEAD_DIM ** -0.5
    w_proj = (w_in[:, :g0] * col_scale).astype(BF16)
    b_proj = (b_in[:g0] * col_scale)[None, :]
    inv_freq = jnp.power(ROPE_THETA, -jnp.arange(32, dtype=F32) * 2.0 / HEAD_DIM)
    invf = jnp.tile(inv_freq, 4)[None, :]
    sgn = jnp.asarray(np.tile(np.repeat(np.float32([-1.0, 1.0]), 32), 2))[None, :]
    step_perm = _step_major_perm(bsz, ts)

    pos_tiles = jnp.transpose(positions.reshape(bsz, seq // ts, ts), (1, 0, 2))
    vp, q, k, v = _proj(x, pos_tiles, w_proj, b_proj, invf, sgn,
                        jnp.asarray(step_perm, dtype=BF16), ts)

    m_pair, q_pair, p_pair, dec = _ssm_params(lam_re, lam_im, log_dt, b_re, b_im, c_re, c_im, d_skip)
    z = _ssm(vp, m_pair, q_pair, p_pair, dec, n_chunks, bsz)

    attn = _attn(q, k, v, _interleave_perms(), _band_bias())

    wg = w_in[:, g0:].astype(BF16)
    bg = b_in[g0:][None, :]
    wglu = jnp.concatenate([w_glu_v, w_glu_g], axis=1).astype(BF16)
    h = _mix(x, z, attn, jnp.asarray(step_perm.T, dtype=BF16), wg, bg, wglu, w_attn_br.astype(BF16),
             w_out.astype(BF16), ln1_g[None, :], ln1_b[None, :], ts)

    cw = jnp.concatenate([conv_w, jnp.zeros((5, 2 * D_FF), F32)], axis=0)
    out = _ffn(h.reshape(t, D_MODEL), w_up.astype(BF16), cw, conv_b[None, :], w_down.astype(BF16),
               ln2_g[None, :], ln2_b[None, :], FFN_TM, seq)
    return out.reshape(bsz, seq, D_MODEL)


def kernel(x, positions, w_in, b_in, ssm_lam_re, ssm_lam_im, ssm_log_dt, ssm_b_re, ssm_b_im, ssm_c_re, ssm_c_im, ssm_d, w_glu_v, w_glu_g, w_attn_br, w_out, ln1_g, ln1_b, w_up, conv_w, conv_b, w_down, ln2_g, ln2_b):
    h = x
    for layer in range(w_in.shape[0]):
        h = _layer(h, positions, w_in[layer], b_in[layer], ssm_lam_re[layer], ssm_lam_im[layer],
                   ssm_log_dt[layer], ssm_b_re[layer], ssm_b_im[layer], ssm_c_re[layer], ssm_c_im[layer],
                   ssm_d[layer], w_glu_v[layer], w_glu_g[layer], w_attn_br[layer], w_out[layer],
                   ln1_g[layer], ln1_b[layer], w_up[layer], conv_w[layer], conv_b[layer], w_down[layer],
                   ln2_g[layer], ln2_b[layer])
    return h
```

```python
import functools
import math

import numpy as np
import jax
import jax.numpy as jnp
from jax import lax
from jax.experimental import pallas as pl
from jax.experimental.pallas import tpu as pltpu

D_MODEL = 1024
SSM_WIDTH = 512
SSM_GROUP = 16
SSM_GROUPS = 32
SSM_STATE = 64
HEAD_DIM = 64
HEADS_PER_GROUP = 4
ATTN_PATTERNS = ((128, 1), (512, 4), (2048, 16))
N_HEADS = 12
ATTN_WIDTH = 768
ATTN_OUT_WIDTH = 256
D_FF = 2816
LN_EPS = 1e-5
NEG_INF = -1e30
ROPE_THETA = 10000.0
DEEPNORM_ALPHA = 2.0 ** 0.25

CHUNK = 16
PAIR_W = 2 * SSM_GROUP * CHUNK
N_PAIRS = SSM_GROUPS // 2
HALF_WIN = 64
QBLK = 128
KBLK = 256
PTILE = 256
PROJ_W = SSM_WIDTH + 3 * ATTN_WIDTH
VMEM_LIMIT = 56 * 1024 * 1024
F32 = jnp.float32
BF16 = jnp.bfloat16


def _dot(a, b):
    return jnp.dot(a, b, preferred_element_type=F32)


def _gelu(x):
    return 0.5 * x * (1.0 + lax.erf(x * np.float32(math.sqrt(0.5))))


def _layer_norm(r, g, b):
    mu = jnp.mean(r, axis=-1, keepdims=True)
    c = r - mu
    var = jnp.mean(c * c, axis=-1, keepdims=True)
    return c * lax.rsqrt(var + LN_EPS) * g + b


def _params(n_axes):
    return pltpu.CompilerParams(dimension_semantics=("arbitrary",) * n_axes, vmem_limit_bytes=VMEM_LIMIT)


def _place(r, n_rows):
    lane_grp = lax.broadcasted_iota(jnp.int32, (n_rows, 128), 1) // 32
    out = [[None] * 4 for _ in range(N_PAIRS)]
    for j in range(4):
        for q in range(4):
            src = [r[(4 * j + m) * n_rows:(4 * j + m + 1) * n_rows, 128 * q:128 * (q + 1)] for m in range(4)]
            rolled = [[s if sh == 0 else pltpu.roll(s, 32 * sh, 1) for sh in range(4)] for s in src]
            for pp in range(4):
                d = rolled[0][(0 - pp) % 4]
                for m in range(1, 4):
                    d = jnp.where(lane_grp == m, rolled[m][(m - pp) % 4], d)
                out[4 * q + pp][j] = d
    return out


def _unplace(zs, n_rows):
    lane_grp = lax.broadcasted_iota(jnp.int32, (n_rows, 128), 1) // 32
    rows = []
    for j in range(4):
        for m in range(4):
            cols = []
            for q in range(4):
                d = None
                for pp in range(4):
                    s = zs[4 * q + pp][j]
                    sh = (pp - m) % 4
                    rl = s if sh == 0 else pltpu.roll(s, 32 * sh, 1)
                    d = rl if d is None else jnp.where(lane_grp == pp, rl, d)
                cols.append(d)
            rows.append(jnp.concatenate(cols, axis=1))
    return jnp.concatenate(rows, axis=0)


def _proj_kernel(x_ref, pos_ref, w_ref, b_ref, invf_ref, sgn_ref, perm_ref, vp_ref, q_ref, k_ref, v_ref, *, bsz, ts):
    tm = bsz * ts
    xb = x_ref[...].reshape(tm, D_MODEL).astype(BF16)
    posf = jnp.broadcast_to(pos_ref[...].astype(F32)[:, None, :], (bsz, ts, ts)).reshape(tm, ts)
    own = (lax.broadcasted_iota(jnp.int32, (tm, ts), 0) & (ts - 1)) == lax.broadcasted_iota(jnp.int32, (tm, ts), 1)
    pos_col = jnp.sum(jnp.where(own, posf, 0.0), axis=1, keepdims=True)
    ang = pos_col * invf_ref[...]
    cos = jnp.cos(ang)
    sin = jnp.sin(ang) * sgn_ref[...]
    low_half = (lax.broadcasted_iota(jnp.int32, (tm, 128), 1) & 32) == 0
    u = _dot(xb, w_ref[:, 0:SSM_WIDTH]) + b_ref[:, 0:SSM_WIDTH]
    r = _dot(perm_ref[...], u.astype(BF16))
    placed = _place(r, tm // CHUNK)
    for p in range(N_PAIRS):
        for j in range(4):
            vp_ref[p, :, 128 * j:128 * (j + 1)] = placed[p][j].astype(BF16)
    for dst, c0 in ((q_ref, SSM_WIDTH), (k_ref, SSM_WIDTH + ATTN_WIDTH)):
        t = _dot(xb, w_ref[:, c0:c0 + ATTN_WIDTH]) + b_ref[:, c0:c0 + ATTN_WIDTH]
        for j in range(ATTN_WIDTH // 128):
            tj = t[:, 128 * j:128 * (j + 1)]
            partner = jnp.where(low_half, pltpu.roll(tj, 96, 1), pltpu.roll(tj, 32, 1))
            dst[:, :, 128 * j:128 * (j + 1)] = (tj * cos + partner * sin).astype(BF16).reshape(bsz, ts, 128)
    c0 = SSM_WIDTH + 2 * ATTN_WIDTH
    v = _dot(xb, w_ref[:, c0:c0 + ATTN_WIDTH]) + b_ref[:, c0:c0 + ATTN_WIDTH]
    v_ref[...] = v.astype(BF16).reshape(bsz, ts, ATTN_WIDTH)


def _proj(x, pos, w, b, invf, sgn, perm, ts):
    bsz, seq, _ = x.shape
    tm = bsz * ts
    rows = tm // CHUNK
    tile = lambda w_: pl.BlockSpec((bsz, ts, w_), lambda i: (0, i, 0))
    fixed = lambda i: (0, 0)
    return pl.pallas_call(
        functools.partial(_proj_kernel, bsz=bsz, ts=ts),
        grid=(seq // ts,),
        in_specs=[tile(D_MODEL), pl.BlockSpec((None, bsz, ts), lambda i: (i, 0, 0)),
                  pl.BlockSpec((D_MODEL, PROJ_W), fixed, pipeline_mode=pl.Buffered(1)),
                  pl.BlockSpec((1, PROJ_W), fixed),
                  pl.BlockSpec((1, 128), fixed),
                  pl.BlockSpec((1, 128), fixed),
                  pl.BlockSpec((tm, tm), fixed, pipeline_mode=pl.Buffered(1))],
        out_specs=[pl.BlockSpec((N_PAIRS, rows, PAIR_W), lambda i: (0, i, 0)),
                   tile(ATTN_WIDTH), tile(ATTN_WIDTH), tile(ATTN_WIDTH)],
        out_shape=[jax.ShapeDtypeStruct((N_PAIRS, seq // CHUNK * bsz, PAIR_W), BF16),
                   jax.ShapeDtypeStruct((bsz, seq, ATTN_WIDTH), BF16),
                   jax.ShapeDtypeStruct((bsz, seq, ATTN_WIDTH), BF16),
                   jax.ShapeDtypeStruct((bsz, seq, ATTN_WIDTH), BF16)],
        compiler_params=_params(1),
        name="proj",
    )(x, pos, w, b, invf, sgn, perm)


def _ssm_kernel(v_ref, m_ref, q_ref, p_ref, a_ref, z_ref, h_ref, *, n_chunks, bsz):
    v = v_ref[...]
    h_ref[...] = _dot(v, q_ref[...])
    dec = a_ref[...]
    afr, afi, abr, abi = (jnp.broadcast_to(dec[i:i + 1, :], (bsz, 128)) for i in range(4))

    def step(k, carry):
        hfr, hfi, hbr, hbi = carry
        rf = pl.ds(pl.multiple_of(k * bsz, bsz), bsz)
        rb = pl.ds(pl.multiple_of((n_chunks - 1 - k) * bsz, bsz), bsz)
        xfr = h_ref[rf, 0:128]
        xfi = h_ref[rf, 128:256]
        xbr = h_ref[rb, 256:384]
        xbi = h_ref[rb, 384:512]
        h_ref[rf, 0:128] = hfr
        h_ref[rf, 128:256] = hfi
        h_ref[rb, 256:384] = hbr
        h_ref[rb, 384:512] = hbi
        return (afr * hfr - afi * hfi + xfr, afr * hfi + afi * hfr + xfi,
                abr * hbr - abi * hbi + xbr, abr * hbi + abi * hbr + xbi)

    zero = jnp.zeros((bsz, 128), F32)
    lax.fori_loop(0, n_chunks, step, (zero, zero, zero, zero), unroll=4)
    y = _dot(v, m_ref[...]) + _dot(h_ref[...].astype(BF16), p_ref[...])
    z_ref[...] = _gelu(y).astype(BF16)


def _ssm(vp, m, q, p, a, n_chunks, bsz):
    rows = n_chunks * bsz
    blk = lambda shape: pl.BlockSpec((None,) + shape, lambda i: (i, 0, 0))
    return pl.pallas_call(
        functools.partial(_ssm_kernel, n_chunks=n_chunks, bsz=bsz),
        grid=(N_PAIRS,),
        in_specs=[blk((rows, PAIR_W)), blk((PAIR_W, PAIR_W)), blk((PAIR_W, PAIR_W)),
                  blk((PAIR_W, PAIR_W)), blk((8, 128))],
        out_specs=blk((rows, PAIR_W)),
        out_shape=jax.ShapeDtypeStruct((N_PAIRS, rows, PAIR_W), BF16),
        scratch_shapes=[pltpu.VMEM((rows, PAIR_W), F32)],
        compiler_params=_params(1),
        name="ssm",
    )(vp, m, q, p, a)


def _ssm_params(lam_re, lam_im, log_dt, b_re, b_im, c_re, c_im, d_skip):
    hp = lax.Precision.HIGHEST
    dt = jnp.exp(log_dt)[..., None]
    xr, xi = lam_re * dt, lam_im * dt
    abar_m1_r = jnp.expm1(xr) * jnp.cos(xi) - 2.0 * jnp.square(jnp.sin(0.5 * xi))
    abar_i = jnp.exp(xr) * jnp.sin(xi)
    den = lam_re * lam_re + lam_im * lam_im
    kr = (abar_m1_r * lam_re + abar_i * lam_im) / den
    ki = (abar_i * lam_re - abar_m1_r * lam_im) / den
    bbr = kr[..., None] * b_re - ki[..., None] * b_im
    bbi = kr[..., None] * b_im + ki[..., None] * b_re
    n = jnp.arange(CHUNK + 1, dtype=F32)[:, None, None, None]
    mag = jnp.exp(n * xr[None])
    pr, pi = mag * jnp.cos(n * xi[None]), mag * jnp.sin(n * xi[None])
    wr = c_re[None] * pr[:, :, :, None, :] - c_im[None] * pi[:, :, :, None, :]
    wi = c_re[None] * pi[:, :, :, None, :] + c_im[None] * pr[:, :, :, None, :]
    kern = jnp.einsum('ndgop,dgpc->ndgoc', jnp.concatenate([wr, -wi], axis=-1),
                      jnp.concatenate([bbr, bbi], axis=-2), precision=hp)
    s_idx = jnp.arange(CHUNK)[:, None]
    t_idx = jnp.arange(CHUNK)[None, :]
    lag_f = jnp.clip(t_idx - s_idx, 0, CHUNK)
    lag_b = jnp.clip(s_idx - t_idx, 0, CHUNK)
    kf = jnp.where((s_idx <= t_idx)[:, :, None, None, None], kern[lag_f, 0], 0.0)
    kb = jnp.where((s_idx >= t_idx)[:, :, None, None, None], kern[lag_b, 1], 0.0)
    eye_t = (s_idx == t_idx).astype(F32)[:, :, None, None, None]
    eye_c = jnp.eye(SSM_GROUP, dtype=F32)[None, None, None]
    skip = eye_t * eye_c * d_skip.reshape(SSM_GROUPS, SSM_GROUP)[None, None, :, :, None]
    toep = jnp.transpose(kf + kb + skip, (2, 0, 4, 1, 3))
    toep = toep.reshape(SSM_GROUPS, CHUNK * SSM_GROUP, CHUNK * SSM_GROUP)
    pw_f = jnp.arange(CHUNK - 1, -1, -1)
    pw_b = jnp.arange(CHUNK)

    def in_map(d, pw):
        ar, ai = pr[pw, d], pi[pw, d]
        qr = ar[..., None] * bbr[d][None] - ai[..., None] * bbi[d][None]
        qi = ar[..., None] * bbi[d][None] + ai[..., None] * bbr[d][None]
        f = lambda z: jnp.transpose(z, (1, 0, 3, 2)).reshape(SSM_GROUPS, CHUNK * SSM_GROUP, SSM_STATE)
        return f(qr), f(qi)

    qfr, qfi = in_map(0, pw_f)
    qbr, qbi = in_map(1, pw_b)
    pw_of = jnp.arange(1, CHUNK + 1)
    pw_ob = jnp.arange(CHUNK, 0, -1)

    def out_map(d, pw):
        g = lambda z: jnp.transpose(z[pw, d], (1, 3, 0, 2)).reshape(SSM_GROUPS, SSM_STATE, CHUNK * SSM_GROUP)
        return g(wr), -g(wi)

    pfr, pfi = out_map(0, pw_of)
    pbr, pbi = out_map(1, pw_ob)

    def pair_cols(parts):
        out = jnp.zeros((N_PAIRS, PAIR_W, 4, 2, SSM_STATE), BF16)
        for pi_, z in enumerate(parts):
            zz = z.astype(BF16).reshape(N_PAIRS, 2, CHUNK * SSM_GROUP, SSM_STATE)
            for gl in range(2):
                out = out.at[:, gl * 256:(gl + 1) * 256, pi_, gl, :].set(zz[:, gl])
        return out.reshape(N_PAIRS, PAIR_W, PAIR_W)

    def pair_rows(parts):
        out = jnp.zeros((N_PAIRS, 4, 2, SSM_STATE, PAIR_W), BF16)
        for pi_, z in enumerate(parts):
            zz = z.astype(BF16).reshape(N_PAIRS, 2, SSM_STATE, CHUNK * SSM_GROUP)
            for gl in range(2):
                out = out.at[:, pi_, gl, :, gl * 256:(gl + 1) * 256].set(zz[:, gl])
        return out.reshape(N_PAIRS, PAIR_W, PAIR_W)

    q_pair = pair_cols([qfr, qfi, qbr, qbi])
    p_pair = pair_rows([pfr, pfi, pbr, pbi])
    tp = toep.astype(BF16).reshape(N_PAIRS, 2, 256, 256)
    m_pair = jnp.zeros((N_PAIRS, PAIR_W, PAIR_W), BF16)
    m_pair = m_pair.at[:, 0:256, 0:256].set(tp[:, 0]).at[:, 256:512, 256:512].set(tp[:, 1])
    new = np.arange(PAIR_W)
    old = ((new // 16) % 2) * 256 + (new // 32) * 16 + new % 16
    sel = np.zeros((PAIR_W, PAIR_W), np.float32)
    sel[new, old] = 1.0
    sel = jnp.asarray(sel, dtype=BF16)
    rows_step_major = lambda z: jnp.einsum('ij,pjk->pik', sel, z, preferred_element_type=BF16)
    cols_step_major = lambda z: jnp.einsum('pjk,lk->pjl', z, sel, preferred_element_type=BF16)
    m_pair = cols_step_major(rows_step_major(m_pair))
    q_pair = rows_step_major(q_pair)
    p_pair = cols_step_major(p_pair)
    dec = jnp.stack([pr[CHUNK, 0], pi[CHUNK, 0], pr[CHUNK, 1], pi[CHUNK, 1]], axis=0)
    dec = jnp.transpose(dec.reshape(4, N_PAIRS, 128), (1, 0, 2))
    dec = jnp.concatenate([dec, jnp.zeros((N_PAIRS, 4, 128), F32)], axis=1)
    return m_pair, q_pair, p_pair, dec


def _banded(q_ref, k_ref, v_ref, bias_ref, o_ref, m_ref, d_ref, *, seq, n_sub):
    lane = lax.broadcasted_iota(jnp.int32, (QBLK, 128), 1)
    low64 = lane < 64
    head0 = low64
    zero = jnp.zeros((QBLK, 128), BF16)
    ones = jnp.ones((KBLK, 128), BF16)
    blocks_per_seq = n_sub // QBLK

    def block(i, carry):
        base = (i // blocks_per_seq) * n_sub
        m0 = (i % blocks_per_seq) * QBLK
        ks = jnp.clip(m0 - HALF_WIN, 0, n_sub - KBLK)
        bias = bias_ref[(m0 - ks) // HALF_WIN]
        qrow = pl.multiple_of(base + m0, QBLK)
        krow = pl.multiple_of(base + ks, HALF_WIN)
        for pi_ in range(2):
            cs = slice(128 * pi_, 128 * (pi_ + 1))
            q2 = q_ref[pl.ds(qrow, QBLK), cs]
            k2 = k_ref[pl.ds(krow, KBLK), cs]
            v2 = jnp.concatenate([v_ref[pl.ds(krow, KBLK), cs], ones], axis=1)
            qq = jnp.concatenate([jnp.where(head0, q2, zero), jnp.where(head0, zero, q2)], axis=0)
            s = lax.dot_general(qq, k2, (((1,), (1,)), ((), ())), preferred_element_type=F32)
            s = s + bias
            mx = jnp.max(s, axis=-1, keepdims=True)
            pv = _dot(jnp.exp(s - mx).astype(BF16), v2)
            rows = pl.ds(qrow, QBLK)
            o_ref[rows, cs] = jnp.where(low64, pv[:QBLK, :128], pv[QBLK:, :128]).astype(o_ref.dtype)
            m_ref[rows, cs] = jnp.where(low64, mx[:QBLK], mx[QBLK:])
            d_ref[rows, cs] = jnp.where(low64, pv[:QBLK, 128:], pv[QBLK:, 128:]).astype(d_ref.dtype)
        return carry

    lax.fori_loop(0, seq // QBLK, block, 0, unroll=16)


def _attn_kernel(q_ref, k_ref, v_ref, perm_ref, bias_ref, out_ref, qd_ref, kd_ref, vd_ref, od_ref, dd_ref, md_ref,
                 num_ref, mx_ref, den_ref, *, seq):
    g = pl.program_id(1)
    n_tiles = seq // PTILE

    def deinterleave(dil, pidx):
        n_loc = PTILE // dil
        n_sub = seq // dil
        pm = perm_ref[pidx]

        def tile(j, carry):
            rows = pl.ds(pl.multiple_of(j * PTILE, PTILE), PTILE)
            for src, dst in ((q_ref, qd_ref), (k_ref, kd_ref), (v_ref, vd_ref)):
                y = _dot(pm, src[rows, :]).astype(BF16)
                for r in range(dil):
                    dst[pl.ds(pl.multiple_of(r * n_sub + j * n_loc, n_loc), n_loc), :] = y[r * n_loc:(r + 1) * n_loc]
            return carry

        lax.fori_loop(0, n_tiles, tile, 0, unroll=4)

    def merge(dil, pidx, last):
        n_loc = PTILE // dil
        n_sub = seq // dil
        pm = perm_ref[pidx]

        def tile(j, carry):
            rows = pl.ds(pl.multiple_of(j * PTILE, PTILE), PTILE)

            def gathered(ref):
                return jnp.concatenate(
                    [ref[pl.ds(pl.multiple_of(r * n_sub + j * n_loc, n_loc), n_loc), :] for r in range(dil)], axis=0)

            o = _dot(pm, gathered(od_ref))
            d_new = _dot(pm, gathered(dd_ref))
            t = gathered(md_ref)
            hi = t.astype(BF16)
            lo = (t - hi.astype(F32)).astype(BF16)
            m_new = _dot(pm, hi) + _dot(pm, lo)
            m_old = mx_ref[rows, :]
            mx = jnp.maximum(m_old, m_new)
            a, b = jnp.exp(m_old - mx), jnp.exp(m_new - mx)
            num = num_ref[rows, :] * a + o * b
            den = den_ref[rows, :] * a + d_new * b
            if last:
                out_ref[rows, :] = (num / den).astype(BF16)
            else:
                num_ref[rows, :] = num
                mx_ref[rows, :] = mx
                den_ref[rows, :] = den
            return carry

        lax.fori_loop(0, n_tiles, tile, 0, unroll=4)

    n_groups = len(ATTN_PATTERNS)
    for gi, (_, dil) in enumerate(ATTN_PATTERNS):
        @pl.when(g == gi)
        def _():
            if gi == 0:
                _banded(q_ref, k_ref, v_ref, bias_ref, num_ref, mx_ref, den_ref, seq=seq, n_sub=seq)
            else:
                deinterleave(dil, 2 * (gi - 1))
                _banded(qd_ref, kd_ref, vd_ref, bias_ref, od_ref, md_ref, dd_ref, seq=seq, n_sub=seq // dil)
                merge(dil, 2 * (gi - 1) + 1, gi == n_groups - 1)


def _band_bias():
    r = np.arange(2 * QBLK)[:, None] & (QBLK - 1)
    j = np.arange(KBLK)[None, :]
    return jnp.asarray(np.stack([np.where(np.abs(j - HALF_WIN * i - r) <= HALF_WIN, 0.0, NEG_INF)
                                 for i in range(3)]), dtype=F32)


def _attn(q, k, v, perms, bias):
    bsz, seq, _ = q.shape
    w = ATTN_OUT_WIDTH
    blk = pl.BlockSpec((None, seq, w), lambda b, g: (b, 0, g))
    return pl.pallas_call(
        functools.partial(_attn_kernel, seq=seq),
        grid=(bsz, len(ATTN_PATTERNS)),
        in_specs=[blk, blk, blk,
                  pl.BlockSpec(perms.shape, lambda b, g: (0, 0, 0), pipeline_mode=pl.Buffered(1)),
                  pl.BlockSpec(bias.shape, lambda b, g: (0, 0, 0), pipeline_mode=pl.Buffered(1))],
        out_specs=pl.BlockSpec((None, seq, w), lambda b, g: (b, 0, 0)),
        out_shape=jax.ShapeDtypeStruct((bsz, seq, w), BF16),
        scratch_shapes=[pltpu.VMEM((seq, w), BF16)] * 5 + [pltpu.VMEM((seq, w), F32)] * 4,
        compiler_params=_params(2),
        name="attn",
    )(q, k, v, perms, bias)


def _interleave_perms():
    mats = []
    for _, dil in ATTN_PATTERNS[1:]:
        n_loc = PTILE // dil
        p = np.zeros((PTILE, PTILE), np.float32)
        pos = np.arange(PTILE)
        p[(pos % dil) * n_loc + pos // dil, pos] = 1.0
        mats.extend([p, p.T])
    return jnp.asarray(np.stack(mats), dtype=BF16)


def _mix_kernel(x_ref, z_ref, a_ref, perm_ref, wg_ref, bg_ref, wglu_ref, wbr_ref, wout_ref, g_ref, b_ref, h_ref,
                *, bsz, ts):
    tm = bsz * ts
    x = x_ref[...].reshape(tm, D_MODEL)
    xb = x.astype(BF16)
    gates = jax.nn.sigmoid(_dot(xb, wg_ref[...]) + bg_ref[...])
    zs = [[z_ref[p, :, 128 * j:128 * (j + 1)].astype(F32) for j in range(4)] for p in range(N_PAIRS)]
    z = _dot(perm_ref[...], _unplace(zs, tm // CHUNK).astype(BF16)).astype(BF16)
    gv = _dot(z, wglu_ref[...])
    ssm_out = gv[:, :D_MODEL] * jax.nn.sigmoid(gv[:, D_MODEL:])
    attn_out = _dot(a_ref[...].reshape(tm, ATTN_OUT_WIDTH), wbr_ref[...])
    mix = gates[:, :D_MODEL] * ssm_out + gates[:, D_MODEL:] * attn_out
    mixed = _dot(mix.astype(BF16), wout_ref[...])
    h = _layer_norm(DEEPNORM_ALPHA * x + mixed, g_ref[...], b_ref[...])
    h_ref[...] = h.reshape(bsz, ts, D_MODEL)


def _mix(x, z, attn, perm_t, wg, bg, wglu, wbr, wout, g, b, ts):
    bsz, seq, _ = x.shape
    tm = bsz * ts
    tile = lambda w_: pl.BlockSpec((bsz, ts, w_), lambda i: (0, i, 0))
    wspec = lambda shape: pl.BlockSpec(shape, lambda i: (0, 0), pipeline_mode=pl.Buffered(1))
    return pl.pallas_call(
        functools.partial(_mix_kernel, bsz=bsz, ts=ts),
        grid=(seq // ts,),
        in_specs=[tile(D_MODEL), pl.BlockSpec((N_PAIRS, tm // CHUNK, PAIR_W), lambda i: (0, i, 0)),
                  tile(ATTN_OUT_WIDTH), wspec((tm, tm)),
                  wspec((D_MODEL, 2 * D_MODEL)), wspec((1, 2 * D_MODEL)), wspec((SSM_WIDTH, 2 * D_MODEL)),
                  wspec((ATTN_OUT_WIDTH, D_MODEL)), wspec((D_MODEL, D_MODEL)), wspec((1, D_MODEL)), wspec((1, D_MODEL))],
        out_specs=tile(D_MODEL),
        out_shape=jax.ShapeDtypeStruct((bsz, seq, D_MODEL), F32),
        compiler_params=_params(1),
        name="mix",
    )(x, z, attn, perm_t, wg, bg, wglu, wbr, wout, g, b)


FFN_CHUNK = 256
FFN_TM = 1024
FFN_HALO = 16


def _ffn_kernel(h_ref, hp_ref, hn_ref, wup_ref, cw_ref, cb_ref, wdn_ref, g_ref, b_ref, o_ref, act_ref,
                *, tm, tiles_per_seq):
    i = pl.program_id(0)
    j = i % tiles_per_seq
    h = h_ref[...]
    hb = h.astype(BF16)
    prev_ok = (j > 0).astype(F32)
    next_ok = (j < tiles_per_seq - 1).astype(F32)
    hpb = (hp_ref[...] * prev_ok).astype(BF16)
    hnb = (hn_ref[...] * next_ok).astype(BF16)
    hext = jnp.concatenate([hpb, hb, hnb], axis=0)
    rows = tm + 2 * FFN_HALO
    mid = slice(FFN_HALO, FFN_HALO + tm)
    for c in range(D_FF // FFN_CHUNK):
        halves = []
        for c0 in (c * FFN_CHUNK, D_FF + c * FFN_CHUNK):
            u = _dot(hext, wup_ref[:, c0:c0 + FFN_CHUNK])
            cw = cw_ref[:, c0:c0 + FFN_CHUNK]
            before = pltpu.roll(u, 1, 0)[mid]
            after = pltpu.roll(u, rows - 1, 0)[mid]
            halves.append(before * cw[0:1] + u[mid] * cw[1:2] + after * cw[2:3] + cb_ref[:, c0:c0 + FFN_CHUNK])
        a, val = halves
        act_ref[:, c * FFN_CHUNK:(c + 1) * FFN_CHUNK] = (_gelu(a) * val).astype(BF16)
    ffn = _dot(act_ref[...], wdn_ref[...])
    o_ref[...] = _layer_norm(DEEPNORM_ALPHA * h + ffn, g_ref[...], b_ref[...])


def _ffn(h, wup, cw, cb, wdn, g, b, tm, seq):
    t = h.shape[0]
    tps = seq // tm
    nblk = t // FFN_HALO
    fixed = lambda i: (0, 0)
    wspec = lambda shape: pl.BlockSpec(shape, fixed, pipeline_mode=pl.Buffered(1))
    return pl.pallas_call(
        functools.partial(_ffn_kernel, tm=tm, tiles_per_seq=tps),
        grid=(t // tm,),
        in_specs=[pl.BlockSpec((tm, D_MODEL), lambda i: (i, 0)),
                  pl.BlockSpec((FFN_HALO, D_MODEL), lambda i: (jnp.maximum(i * (tm // FFN_HALO) - 1, 0), 0)),
                  pl.BlockSpec((FFN_HALO, D_MODEL), lambda i: (jnp.minimum((i + 1) * (tm // FFN_HALO), nblk - 1), 0)),
                  wspec((D_MODEL, 2 * D_FF)), wspec((8, 2 * D_FF)), wspec((1, 2 * D_FF)),
                  wspec((D_FF, D_MODEL)), wspec((1, D_MODEL)), wspec((1, D_MODEL))],
        out_specs=pl.BlockSpec((tm, D_MODEL), lambda i: (i, 0)),
        out_shape=jax.ShapeDtypeStruct((t, D_MODEL), F32),
        scratch_shapes=[pltpu.VMEM((tm, D_FF), BF16)],
        compiler_params=_params(1),
        name="ffn",
    )(h, h, h, wup, cw, cb, wdn, g, b)


def _step_major_perm(bsz, ts):
    p = np.zeros((bsz * ts, bsz * ts), np.float32)
    b, k, s = np.meshgrid(np.arange(bsz), np.arange(ts // CHUNK), np.arange(CHUNK), indexing="ij")
    p[(s * (ts // CHUNK) + k) * bsz + b, (b * (ts // CHUNK) + k) * CHUNK + s] = 1.0
    return p


def _layer(x, positions, w_in, b_in, lam_re, lam_im, log_dt, b_re, b_im, c_re, c_im, d_skip,
           w_glu_v, w_glu_g, w_attn_br, w_out, ln1_g, ln1_b, w_up, conv_w, conv_b, w_down, ln2_g, ln2_b):
    bsz, seq, _ = x.shape
    t = bsz * seq
    ts = 32
    n_chunks = seq // CHUNK

    q0, k0, v0, g0 = SSM_WIDTH, SSM_WIDTH + ATTN_WIDTH, SSM_WIDTH + 2 * ATTN_WIDTH, SSM_WIDTH + 3 * ATTN_WIDTH
    col_scale = np.ones((g0,), np.float32)
    col_scale[q0:k0] = HEAD_DIM ** -0.5
    w_proj = (w_in[:, :g0] * col_scale).astype(BF16)
    b_proj = (b_in[:g0] * col_scale)[None, :]
    inv_freq = jnp.power(ROPE_THETA, -jnp.arange(32, dtype=F32) * 2.0 / HEAD_DIM)
    invf = jnp.tile(inv_freq, 4)[None, :]
    sgn = jnp.asarray(np.tile(np.repeat(np.float32([-1.0, 1.0]), 32), 2))[None, :]
    step_perm = _step_major_perm(bsz, ts)

    pos_tiles = jnp.transpose(positions.reshape(bsz, seq // ts, ts), (1, 0, 2))
    vp, q, k, v = _proj(x, pos_tiles, w_proj, b_proj, invf, sgn,
                        jnp.asarray(step_perm, dtype=BF16), ts)

    m_pair, q_pair, p_pair, dec = _ssm_params(lam_re, lam_im, log_dt, b_re, b_im, c_re, c_im, d_skip)
    z = _ssm(vp, m_pair, q_pair, p_pair, dec, n_chunks, bsz)

    attn = _attn(q, k, v, _interleave_perms(), _band_bias())

    wg = w_in[:, g0:].astype(BF16)
    bg = b_in[g0:][None, :]
    wglu = jnp.concatenate([w_glu_v, w_glu_g], axis=1).astype(BF16)
    h = _mix(x, z, attn, jnp.asarray(step_perm.T, dtype=BF16), wg, bg, wglu, w_attn_br.astype(BF16),
             w_out.astype(BF16), ln1_g[None, :], ln1_b[None, :], ts)

    cw = jnp.concatenate([conv_w, jnp.zeros((5, 2 * D_FF), F32)], axis=0)
    out = _ffn(h.reshape(t, D_MODEL), w_up.astype(BF16), cw, conv_b[None, :], w_down.astype(BF16),
               ln2_g[None, :], ln2_b[None, :], FFN_TM, seq)
    return out.reshape(bsz, seq, D_MODEL)


def kernel(x, positions, w_in, b_in, ssm_lam_re, ssm_lam_im, ssm_log_dt, ssm_b_re, ssm_b_im, ssm_c_re, ssm_c_im, ssm_d, w_glu_v, w_glu_g, w_attn_br, w_out, ln1_g, ln1_b, w_up, conv_w, conv_b, w_down, ln2_g, ln2_b):
    h = x
    for layer in range(w_in.shape[0]):
        h = _layer(h, positions, w_in[layer], b_in[layer], ssm_lam_re[layer], ssm_lam_im[layer],
                   ssm_log_dt[layer], ssm_b_re[layer], ssm_b_im[layer], ssm_c_re[layer], ssm_c_im[layer],
                   ssm_d[layer], w_glu_v[layer], w_glu_g[layer], w_attn_br[layer], w_out[layer],
                   ln1_g[layer], ln1_b[layer], w_up[layer], conv_w[layer], conv_b[layer], w_down[layer],
                   ln2_g[layer], ln2_b[layer])
    return h
```

```python
import functools
import math

import numpy as np
import jax
import jax.numpy as jnp
from jax import lax
from jax.experimental import pallas as pl
from jax.experimental.pallas import tpu as pltpu

D_MODEL = 1024
SSM_WIDTH = 512
SSM_GROUP = 16
SSM_GROUPS = 32
SSM_STATE = 64
HEAD_DIM = 64
HEADS_PER_GROUP = 4
ATTN_PATTERNS = ((128, 1), (512, 4), (2048, 16))
N_HEADS = 12
ATTN_WIDTH = 768
ATTN_OUT_WIDTH = 256
D_FF = 2816
LN_EPS = 1e-5
NEG_INF = -1e30
ROPE_THETA = 10000.0
DEEPNORM_ALPHA = 2.0 ** 0.25

CHUNK = 16
PAIR_W = 2 * SSM_GROUP * CHUNK
N_PAIRS = SSM_GROUPS // 2
LAG_LANES = 1024
HALF_WIN = 64
QBLK = 128
KBLK = 256
PTILE = 256
PROJ_W = SSM_WIDTH + 3 * ATTN_WIDTH
VMEM_LIMIT = 56 * 1024 * 1024
F32 = jnp.float32
BF16 = jnp.bfloat16


def _dot(a, b):
    return jnp.dot(a, b, preferred_element_type=F32)


def _gelu(x):
    return 0.5 * x * (1.0 + lax.erf(x * np.float32(math.sqrt(0.5))))


def _layer_norm(r, g, b):
    mu = jnp.mean(r, axis=-1, keepdims=True)
    c = r - mu
    var = jnp.mean(c * c, axis=-1, keepdims=True)
    return c * lax.rsqrt(var + LN_EPS) * g + b


def _params(n_axes):
    return pltpu.CompilerParams(dimension_semantics=("arbitrary",) * n_axes, vmem_limit_bytes=VMEM_LIMIT)


def _place(r, n_rows):
    lane_grp = lax.broadcasted_iota(jnp.int32, (n_rows, 128), 1) // 32
    out = [[None] * 4 for _ in range(N_PAIRS)]
    for j in range(4):
        for q in range(4):
            src = [r[(4 * j + m) * n_rows:(4 * j + m + 1) * n_rows, 128 * q:128 * (q + 1)] for m in range(4)]
            rolled = [[s if sh == 0 else pltpu.roll(s, 32 * sh, 1) for sh in range(4)] for s in src]
            for pp in range(4):
                d = rolled[0][(0 - pp) % 4]
                for m in range(1, 4):
                    d = jnp.where(lane_grp == m, rolled[m][(m - pp) % 4], d)
                out[4 * q + pp][j] = d
    return out


def _unplace(zs, n_rows):
    lane_grp = lax.broadcasted_iota(jnp.int32, (n_rows, 128), 1) // 32
    rows = []
    for j in range(4):
        for m in range(4):
            cols = []
            for q in range(4):
                d = None
                for pp in range(4):
                    s = zs[4 * q + pp][j]
                    sh = (pp - m) % 4
                    rl = s if sh == 0 else pltpu.roll(s, 32 * sh, 1)
                    d = rl if d is None else jnp.where(lane_grp == pp, rl, d)
                cols.append(d)
            rows.append(jnp.concatenate(cols, axis=1))
    return jnp.concatenate(rows, axis=0)


def _proj_kernel(x_ref, pos_ref, w_ref, b_ref, invf_ref, sgn_ref, perm_ref, vp_ref, q_ref, k_ref, v_ref, *, bsz, ts):
    tm = bsz * ts
    xb = x_ref[...].reshape(tm, D_MODEL).astype(BF16)
    posf = jnp.broadcast_to(pos_ref[...].astype(F32)[:, None, :], (bsz, ts, ts)).reshape(tm, ts)
    own = (lax.broadcasted_iota(jnp.int32, (tm, ts), 0) & (ts - 1)) == lax.broadcasted_iota(jnp.int32, (tm, ts), 1)
    pos_col = jnp.sum(jnp.where(own, posf, 0.0), axis=1, keepdims=True)
    ang = pos_col * invf_ref[...]
    cos = jnp.cos(ang)
    sin = jnp.sin(ang) * sgn_ref[...]
    low_half = (lax.broadcasted_iota(jnp.int32, (tm, 128), 1) & 32) == 0
    u = _dot(xb, w_ref[:, 0:SSM_WIDTH]) + b_ref[:, 0:SSM_WIDTH]
    r = _dot(perm_ref[...], u.astype(BF16))
    placed = _place(r, tm // CHUNK)
    for p in range(N_PAIRS):
        for j in range(4):
            vp_ref[p, :, 128 * j:128 * (j + 1)] = placed[p][j].astype(BF16)
    for dst, c0 in ((q_ref, SSM_WIDTH), (k_ref, SSM_WIDTH + ATTN_WIDTH)):
        t = _dot(xb, w_ref[:, c0:c0 + ATTN_WIDTH]) + b_ref[:, c0:c0 + ATTN_WIDTH]
        for j in range(ATTN_WIDTH // 128):
            tj = t[:, 128 * j:128 * (j + 1)]
            partner = jnp.where(low_half, pltpu.roll(tj, 96, 1), pltpu.roll(tj, 32, 1))
            dst[:, :, 128 * j:128 * (j + 1)] = (tj * cos + partner * sin).astype(BF16).reshape(bsz, ts, 128)
    c0 = SSM_WIDTH + 2 * ATTN_WIDTH
    v = _dot(xb, w_ref[:, c0:c0 + ATTN_WIDTH]) + b_ref[:, c0:c0 + ATTN_WIDTH]
    v_ref[...] = v.astype(BF16).reshape(bsz, ts, ATTN_WIDTH)


def _proj(x, pos, w, b, invf, sgn, perm, ts):
    bsz, seq, _ = x.shape
    tm = bsz * ts
    rows = tm // CHUNK
    tile = lambda w_: pl.BlockSpec((bsz, ts, w_), lambda i: (0, i, 0))
    fixed = lambda i: (0, 0)
    return pl.pallas_call(
        functools.partial(_proj_kernel, bsz=bsz, ts=ts),
        grid=(seq // ts,),
        in_specs=[tile(D_MODEL), pl.BlockSpec((None, bsz, ts), lambda i: (i, 0, 0)),
                  pl.BlockSpec((D_MODEL, PROJ_W), fixed, pipeline_mode=pl.Buffered(1)),
                  pl.BlockSpec((1, PROJ_W), fixed),
                  pl.BlockSpec((1, 128), fixed),
                  pl.BlockSpec((1, 128), fixed),
                  pl.BlockSpec((tm, tm), fixed, pipeline_mode=pl.Buffered(1))],
        out_specs=[pl.BlockSpec((N_PAIRS, rows, PAIR_W), lambda i: (0, i, 0)),
                   tile(ATTN_WIDTH), tile(ATTN_WIDTH), tile(ATTN_WIDTH)],
        out_shape=[jax.ShapeDtypeStruct((N_PAIRS, seq // CHUNK * bsz, PAIR_W), BF16),
                   jax.ShapeDtypeStruct((bsz, seq, ATTN_WIDTH), BF16),
                   jax.ShapeDtypeStruct((bsz, seq, ATTN_WIDTH), BF16),
                   jax.ShapeDtypeStruct((bsz, seq, ATTN_WIDTH), BF16)],
        compiler_params=_params(1),
        name="proj",
    )(x, pos, w, b, invf, sgn, perm)


def _ssm_kernel(v_ref, lag_ref, q_ref, p_ref, a_ref, z_ref, h_ref, m_ref, *, n_chunks, bsz):
    lags = lag_ref[...]
    for s in range(CHUNK):
        off = 32 * (CHUNK - 1 - s)
        win = lags[:, :PAIR_W] if off == 0 else pltpu.roll(lags, LAG_LANES - off, 1)[:, :PAIR_W]
        m_ref[32 * s:32 * (s + 1), :] = win.astype(BF16)
    v = v_ref[...]
    h_ref[...] = _dot(v, q_ref[...])
    dec = a_ref[...]
    afr, afi, abr, abi = (jnp.broadcast_to(dec[i:i + 1, :], (bsz, 128)) for i in range(4))

    def step(k, carry):
        hfr, hfi, hbr, hbi = carry
        rf = pl.ds(pl.multiple_of(k * bsz, bsz), bsz)
        rb = pl.ds(pl.multiple_of((n_chunks - 1 - k) * bsz, bsz), bsz)
        xfr = h_ref[rf, 0:128]
        xfi = h_ref[rf, 128:256]
        xbr = h_ref[rb, 256:384]
        xbi = h_ref[rb, 384:512]
        h_ref[rf, 0:128] = hfr
        h_ref[rf, 128:256] = hfi
        h_ref[rb, 256:384] = hbr
        h_ref[rb, 384:512] = hbi
        return (afr * hfr - afi * hfi + xfr, afr * hfi + afi * hfr + xfi,
                abr * hbr - abi * hbi + xbr, abr * hbi + abi * hbr + xbi)

    zero = jnp.zeros((bsz, 128), F32)
    lax.fori_loop(0, n_chunks, step, (zero, zero, zero, zero), unroll=4)
    y = _dot(v, m_ref[...]) + _dot(h_ref[...].astype(BF16), p_ref[...])
    z_ref[...] = _gelu(y).astype(BF16)


def _ssm(vp, lags, q, p, a, n_chunks, bsz):
    rows = n_chunks * bsz
    blk = lambda shape: pl.BlockSpec((None,) + shape, lambda i: (i, 0, 0))
    return pl.pallas_call(
        functools.partial(_ssm_kernel, n_chunks=n_chunks, bsz=bsz),
        grid=(N_PAIRS,),
        in_specs=[blk((rows, PAIR_W)), blk((2 * SSM_GROUP, LAG_LANES)), blk((PAIR_W, PAIR_W)),
                  blk((PAIR_W, PAIR_W)), blk((8, 128))],
        out_specs=blk((rows, PAIR_W)),
        out_shape=jax.ShapeDtypeStruct((N_PAIRS, rows, PAIR_W), BF16),
        scratch_shapes=[pltpu.VMEM((rows, PAIR_W), F32), pltpu.VMEM((PAIR_W, PAIR_W), BF16)],
        compiler_params=_params(1),
        name="ssm",
    )(vp, lags, q, p, a)


def _ssm_params(lam_re, lam_im, log_dt, b_re, b_im, c_re, c_im, d_skip):
    hp = lax.Precision.HIGHEST
    dt = jnp.exp(log_dt)[..., None]
    xr, xi = lam_re * dt, lam_im * dt
    abar_m1_r = jnp.expm1(xr) * jnp.cos(xi) - 2.0 * jnp.square(jnp.sin(0.5 * xi))
    abar_i = jnp.exp(xr) * jnp.sin(xi)
    den = lam_re * lam_re + lam_im * lam_im
    kr = (abar_m1_r * lam_re + abar_i * lam_im) / den
    ki = (abar_i * lam_re - abar_m1_r * lam_im) / den
    bbr = kr[..., None] * b_re - ki[..., None] * b_im
    bbi = kr[..., None] * b_im + ki[..., None] * b_re
    n = jnp.arange(CHUNK + 1, dtype=F32)[:, None, None, None]
    mag = jnp.exp(n * xr[None])
    pr, pi = mag * jnp.cos(n * xi[None]), mag * jnp.sin(n * xi[None])
    wr = c_re[None] * pr[:, :, :, None, :] - c_im[None] * pi[:, :, :, None, :]
    wi = c_re[None] * pi[:, :, :, None, :] + c_im[None] * pr[:, :, :, None, :]
    kern = jnp.einsum('ndgop,dgpc->ndgoc', jnp.concatenate([wr, -wi], axis=-1),
                      jnp.concatenate([bbr, bbi], axis=-2), precision=hp)
    skip = jnp.eye(SSM_GROUP, dtype=F32)[None] * d_skip.reshape(SSM_GROUPS, SSM_GROUP)[:, :, None]
    lags = jnp.concatenate([kern[CHUNK - 1:0:-1, 1], (kern[0, 0] + kern[0, 1] + skip)[None], kern[1:CHUNK, 0]], axis=0)
    lags = jnp.transpose(lags.reshape(2 * CHUNK - 1, N_PAIRS, 2, SSM_GROUP, SSM_GROUP), (1, 2, 4, 0, 3))
    lag_pair = lags[:, :, :, :, None, :] * jnp.eye(2, dtype=F32)[None, :, None, None, :, None]
    lag_pair = lag_pair.reshape(N_PAIRS, 2 * SSM_GROUP, (2 * CHUNK - 1) * 2 * SSM_GROUP)
    lag_pair = jnp.pad(lag_pair, ((0, 0), (0, 0), (0, LAG_LANES - lag_pair.shape[-1])))
    pw_f = jnp.arange(CHUNK - 1, -1, -1)
    pw_b = jnp.arange(CHUNK)

    def in_map(d, pw):
        ar, ai = pr[pw, d], pi[pw, d]
        qr = ar[..., None] * bbr[d][None] - ai[..., None] * bbi[d][None]
        qi = ar[..., None] * bbi[d][None] + ai[..., None] * bbr[d][None]
        f = lambda z: jnp.transpose(z, (1, 0, 3, 2)).reshape(SSM_GROUPS, CHUNK * SSM_GROUP, SSM_STATE)
        return f(qr), f(qi)

    qfr, qfi = in_map(0, pw_f)
    qbr, qbi = in_map(1, pw_b)
    pw_of = jnp.arange(1, CHUNK + 1)
    pw_ob = jnp.arange(CHUNK, 0, -1)

    def out_map(d, pw):
        g = lambda z: jnp.transpose(z[pw, d], (1, 3, 0, 2)).reshape(SSM_GROUPS, SSM_STATE, CHUNK * SSM_GROUP)
        return g(wr), -g(wi)

    pfr, pfi = out_map(0, pw_of)
    pbr, pbi = out_map(1, pw_ob)

    def pair_cols(parts):
        out = jnp.zeros((N_PAIRS, PAIR_W, 4, 2, SSM_STATE), BF16)
        for pi_, z in enumerate(parts):
            zz = z.astype(BF16).reshape(N_PAIRS, 2, CHUNK * SSM_GROUP, SSM_STATE)
            for gl in range(2):
                out = out.at[:, gl * 256:(gl + 1) * 256, pi_, gl, :].set(zz[:, gl])
        return out.reshape(N_PAIRS, PAIR_W, PAIR_W)

    def pair_rows(parts):
        out = jnp.zeros((N_PAIRS, 4, 2, SSM_STATE, PAIR_W), BF16)
        for pi_, z in enumerate(parts):
            zz = z.astype(BF16).reshape(N_PAIRS, 2, SSM_STATE, CHUNK * SSM_GROUP)
            for gl in range(2):
                out = out.at[:, pi_, gl, :, gl * 256:(gl + 1) * 256].set(zz[:, gl])
        return out.reshape(N_PAIRS, PAIR_W, PAIR_W)

    q_pair = pair_cols([qfr, qfi, qbr, qbi])
    p_pair = pair_rows([pfr, pfi, pbr, pbi])
    new = np.arange(PAIR_W)
    old = ((new // 16) % 2) * 256 + (new // 32) * 16 + new % 16
    sel = np.zeros((PAIR_W, PAIR_W), np.float32)
    sel[new, old] = 1.0
    sel = jnp.asarray(sel, dtype=BF16)
    rows_step_major = lambda z: jnp.einsum('ij,pjk->pik', sel, z, preferred_element_type=BF16)
    cols_step_major = lambda z: jnp.einsum('pjk,lk->pjl', z, sel, preferred_element_type=BF16)
    q_pair = rows_step_major(q_pair)
    p_pair = cols_step_major(p_pair)
    dec = jnp.stack([pr[CHUNK, 0], pi[CHUNK, 0], pr[CHUNK, 1], pi[CHUNK, 1]], axis=0)
    dec = jnp.transpose(dec.reshape(4, N_PAIRS, 128), (1, 0, 2))
    dec = jnp.concatenate([dec, jnp.zeros((N_PAIRS, 4, 128), F32)], axis=1)
    return lag_pair, q_pair, p_pair, dec


def _banded(q_ref, k_ref, v_ref, bias_ref, o_ref, m_ref, d_ref, *, seq, n_sub):
    lane = lax.broadcasted_iota(jnp.int32, (QBLK, 128), 1)
    low64 = lane < 64
    head0 = low64
    zero = jnp.zeros((QBLK, 128), BF16)
    ones = jnp.ones((KBLK, 128), BF16)
    blocks_per_seq = n_sub // QBLK

    def block(i, carry):
        base = (i // blocks_per_seq) * n_sub
        m0 = (i % blocks_per_seq) * QBLK
        ks = jnp.clip(m0 - HALF_WIN, 0, n_sub - KBLK)
        bias = bias_ref[(m0 - ks) // HALF_WIN]
        qrow = pl.multiple_of(base + m0, QBLK)
        krow = pl.multiple_of(base + ks, HALF_WIN)
        for pi_ in range(2):
            cs = slice(128 * pi_, 128 * (pi_ + 1))
            q2 = q_ref[pl.ds(qrow, QBLK), cs]
            k2 = k_ref[pl.ds(krow, KBLK), cs]
            v2 = jnp.concatenate([v_ref[pl.ds(krow, KBLK), cs], ones], axis=1)
            qq = jnp.concatenate([jnp.where(head0, q2, zero), jnp.where(head0, zero, q2)], axis=0)
            s = lax.dot_general(qq, k2, (((1,), (1,)), ((), ())), preferred_element_type=F32)
            s = s + bias
            mx = jnp.max(s, axis=-1, keepdims=True)
            pv = _dot(jnp.exp(s - mx).astype(BF16), v2)
            rows = pl.ds(qrow, QBLK)
            o_ref[rows, cs] = jnp.where(low64, pv[:QBLK, :128], pv[QBLK:, :128]).astype(o_ref.dtype)
            m_ref[rows, cs] = jnp.where(low64, mx[:QBLK], mx[QBLK:])
            d_ref[rows, cs] = jnp.where(low64, pv[:QBLK, 128:], pv[QBLK:, 128:]).astype(d_ref.dtype)
        return carry

    lax.fori_loop(0, seq // QBLK, block, 0, unroll=16)


def _attn_kernel(q_ref, k_ref, v_ref, perm_ref, bias_ref, out_ref, qd_ref, kd_ref, vd_ref, od_ref, dd_ref, md_ref,
                 num_ref, mx_ref, den_ref, *, seq):
    g = pl.program_id(1)
    n_tiles = seq // PTILE

    def deinterleave(dil, pidx):
        n_loc = PTILE // dil
        n_sub = seq // dil
        pm = perm_ref[pidx]

        def tile(j, carry):
            rows = pl.ds(pl.multiple_of(j * PTILE, PTILE), PTILE)
            for src, dst in ((q_ref, qd_ref), (k_ref, kd_ref), (v_ref, vd_ref)):
                y = _dot(pm, src[rows, :]).astype(BF16)
                for r in range(dil):
                    dst[pl.ds(pl.multiple_of(r * n_sub + j * n_loc, n_loc), n_loc), :] = y[r * n_loc:(r + 1) * n_loc]
            return carry

        lax.fori_loop(0, n_tiles, tile, 0, unroll=4)

    def merge(dil, pidx, last):
        n_loc = PTILE // dil
        n_sub = seq // dil
        pm = perm_ref[pidx]

        def tile(j, carry):
            rows = pl.ds(pl.multiple_of(j * PTILE, PTILE), PTILE)

            def gathered(ref):
                return jnp.concatenate(
                    [ref[pl.ds(pl.multiple_of(r * n_sub + j * n_loc, n_loc), n_loc), :] for r in range(dil)], axis=0)

            o = _dot(pm, gathered(od_ref))
            d_new = _dot(pm, gathered(dd_ref))
            t = gathered(md_ref)
            hi = t.astype(BF16)
            lo = (t - hi.astype(F32)).astype(BF16)
            m_new = _dot(pm, hi) + _dot(pm, lo)
            m_old = mx_ref[rows, :]
            mx = jnp.maximum(m_old, m_new)
            a, b = jnp.exp(m_old - mx), jnp.exp(m_new - mx)
            num = num_ref[rows, :] * a + o * b
            den = den_ref[rows, :] * a + d_new * b
            if last:
                out_ref[rows, :] = (num / den).astype(BF16)
            else:
                num_ref[rows, :] = num
                mx_ref[rows, :] = mx
                den_ref[rows, :] = den
            return carry

        lax.fori_loop(0, n_tiles, tile, 0, unroll=4)

    n_groups = len(ATTN_PATTERNS)
    for gi, (_, dil) in enumerate(ATTN_PATTERNS):
        @pl.when(g == gi)
        def _():
            if gi == 0:
                _banded(q_ref, k_ref, v_ref, bias_ref, num_ref, mx_ref, den_ref, seq=seq, n_sub=seq)
            else:
                deinterleave(dil, 2 * (gi - 1))
                _banded(qd_ref, kd_ref, vd_ref, bias_ref, od_ref, md_ref, dd_ref, seq=seq, n_sub=seq // dil)
                merge(dil, 2 * (gi - 1) + 1, gi == n_groups - 1)


def _band_bias():
    r = np.arange(2 * QBLK)[:, None] & (QBLK - 1)
    j = np.arange(KBLK)[None, :]
    return jnp.asarray(np.stack([np.where(np.abs(j - HALF_WIN * i - r) <= HALF_WIN, 0.0, NEG_INF)
                                 for i in range(3)]), dtype=F32)


def _attn(q, k, v, perms, bias):
    bsz, seq, _ = q.shape
    w = ATTN_OUT_WIDTH
    blk = pl.BlockSpec((None, seq, w), lambda b, g: (b, 0, g))
    return pl.pallas_call(
        functools.partial(_attn_kernel, seq=seq),
        grid=(bsz, len(ATTN_PATTERNS)),
        in_specs=[blk, blk, blk,
                  pl.BlockSpec(perms.shape, lambda b, g: (0, 0, 0), pipeline_mode=pl.Buffered(1)),
                  pl.BlockSpec(bias.shape, lambda b, g: (0, 0, 0), pipeline_mode=pl.Buffered(1))],
        out_specs=pl.BlockSpec((None, seq, w), lambda b, g: (b, 0, 0)),
        out_shape=jax.ShapeDtypeStruct((bsz, seq, w), BF16),
        scratch_shapes=[pltpu.VMEM((seq, w), BF16)] * 5 + [pltpu.VMEM((seq, w), F32)] * 4,
        compiler_params=_params(2),
        name="attn",
    )(q, k, v, perms, bias)


def _interleave_perms():
    mats = []
    for _, dil in ATTN_PATTERNS[1:]:
        n_loc = PTILE // dil
        p = np.zeros((PTILE, PTILE), np.float32)
        pos = np.arange(PTILE)
        p[(pos % dil) * n_loc + pos // dil, pos] = 1.0
        mats.extend([p, p.T])
    return jnp.asarray(np.stack(mats), dtype=BF16)


def _mix_kernel(x_ref, z_ref, a_ref, perm_ref, wg_ref, bg_ref, wglu_ref, wbr_ref, wout_ref, g_ref, b_ref, h_ref,
                *, bsz, ts):
    tm = bsz * ts
    x = x_ref[...].reshape(tm, D_MODEL)
    xb = x.astype(BF16)
    gates = jax.nn.sigmoid(_dot(xb, wg_ref[...]) + bg_ref[...])
    zs = [[z_ref[p, :, 128 * j:128 * (j + 1)].astype(F32) for j in range(4)] for p in range(N_PAIRS)]
    z = _dot(perm_ref[...], _unplace(zs, tm // CHUNK).astype(BF16)).astype(BF16)
    gv = _dot(z, wglu_ref[...])
    ssm_out = gv[:, :D_MODEL] * jax.nn.sigmoid(gv[:, D_MODEL:])
    attn_out = _dot(a_ref[...].reshape(tm, ATTN_OUT_WIDTH), wbr_ref[...])
    mix = gates[:, :D_MODEL] * ssm_out + gates[:, D_MODEL:] * attn_out
    mixed = _dot(mix.astype(BF16), wout_ref[...])
    h = _layer_norm(DEEPNORM_ALPHA * x + mixed, g_ref[...], b_ref[...])
    h_ref[...] = h.reshape(bsz, ts, D_MODEL)


def _mix(x, z, attn, perm_t, wg, bg, wglu, wbr, wout, g, b, ts):
    bsz, seq, _ = x.shape
    tm = bsz * ts
    tile = lambda w_: pl.BlockSpec((bsz, ts, w_), lambda i: (0, i, 0))
    wspec = lambda shape: pl.BlockSpec(shape, lambda i: (0, 0), pipeline_mode=pl.Buffered(1))
    return pl.pallas_call(
        functools.partial(_mix_kernel, bsz=bsz, ts=ts),
        grid=(seq // ts,),
        in_specs=[tile(D_MODEL), pl.BlockSpec((N_PAIRS, tm // CHUNK, PAIR_W), lambda i: (0, i, 0)),
                  tile(ATTN_OUT_WIDTH), wspec((tm, tm)),
                  wspec((D_MODEL, 2 * D_MODEL)), wspec((1, 2 * D_MODEL)), wspec((SSM_WIDTH, 2 * D_MODEL)),
                  wspec((ATTN_OUT_WIDTH, D_MODEL)), wspec((D_MODEL, D_MODEL)), wspec((1, D_MODEL)), wspec((1, D_MODEL))],
        out_specs=tile(D_MODEL),
        out_shape=jax.ShapeDtypeStruct((bsz, seq, D_MODEL), F32),
        compiler_params=_params(1),
        name="mix",
    )(x, z, attn, perm_t, wg, bg, wglu, wbr, wout, g, b)


FFN_CHUNK = 256
FFN_TM = 1024
FFN_HALO = 16


def _ffn_kernel(h_ref, hp_ref, hn_ref, wup_ref, cw_ref, cb_ref, wdn_ref, g_ref, b_ref, o_ref, act_ref,
                *, tm, tiles_per_seq):
    i = pl.program_id(0)
    j = i % tiles_per_seq
    h = h_ref[...]
    hb = h.astype(BF16)
    prev_ok = (j > 0).astype(F32)
    next_ok = (j < tiles_per_seq - 1).astype(F32)
    hpb = (hp_ref[...] * prev_ok).astype(BF16)
    hnb = (hn_ref[...] * next_ok).astype(BF16)
    hext = jnp.concatenate([hpb, hb, hnb], axis=0)
    rows = tm + 2 * FFN_HALO
    mid = slice(FFN_HALO, FFN_HALO + tm)
    for c in range(D_FF // FFN_CHUNK):
        halves = []
        for c0 in (c * FFN_CHUNK, D_FF + c * FFN_CHUNK):
            u = _dot(hext, wup_ref[:, c0:c0 + FFN_CHUNK])
            cw = cw_ref[:, c0:c0 + FFN_CHUNK]
            before = pltpu.roll(u, 1, 0)[mid]
            after = pltpu.roll(u, rows - 1, 0)[mid]
            halves.append(before * cw[0:1] + u[mid] * cw[1:2] + after * cw[2:3] + cb_ref[:, c0:c0 + FFN_CHUNK])
        a, val = halves
        act_ref[:, c * FFN_CHUNK:(c + 1) * FFN_CHUNK] = (_gelu(a) * val).astype(BF16)
    ffn = _dot(act_ref[...], wdn_ref[...])
    o_ref[...] = _layer_norm(DEEPNORM_ALPHA * h + ffn, g_ref[...], b_ref[...])


def _ffn(h, wup, cw, cb, wdn, g, b, tm, seq):
    t = h.shape[0]
    tps = seq // tm
    nblk = t // FFN_HALO
    fixed = lambda i: (0, 0)
    wspec = lambda shape: pl.BlockSpec(shape, fixed, pipeline_mode=pl.Buffered(1))
    return pl.pallas_call(
        functools.partial(_ffn_kernel, tm=tm, tiles_per_seq=tps),
        grid=(t // tm,),
        in_specs=[pl.BlockSpec((tm, D_MODEL), lambda i: (i, 0)),
                  pl.BlockSpec((FFN_HALO, D_MODEL), lambda i: (jnp.maximum(i * (tm // FFN_HALO) - 1, 0), 0)),
                  pl.BlockSpec((FFN_HALO, D_MODEL), lambda i: (jnp.minimum((i + 1) * (tm // FFN_HALO), nblk - 1), 0)),
                  wspec((D_MODEL, 2 * D_FF)), wspec((8, 2 * D_FF)), wspec((1, 2 * D_FF)),
                  wspec((D_FF, D_MODEL)), wspec((1, D_MODEL)), wspec((1, D_MODEL))],
        out_specs=pl.BlockSpec((tm, D_MODEL), lambda i: (i, 0)),
        out_shape=jax.ShapeDtypeStruct((t, D_MODEL), F32),
        scratch_shapes=[pltpu.VMEM((tm, D_FF), BF16)],
        compiler_params=_params(1),
        name="ffn",
    )(h, h, h, wup, cw, cb, wdn, g, b)


def _step_major_perm(bsz, ts):
    p = np.zeros((bsz * ts, bsz * ts), np.float32)
    b, k, s = np.meshgrid(np.arange(bsz), np.arange(ts // CHUNK), np.arange(CHUNK), indexing="ij")
    p[(s * (ts // CHUNK) + k) * bsz + b, (b * (ts // CHUNK) + k) * CHUNK + s] = 1.0
    return p


def _layer(x, positions, w_in, b_in, lam_re, lam_im, log_dt, b_re, b_im, c_re, c_im, d_skip,
           w_glu_v, w_glu_g, w_attn_br, w_out, ln1_g, ln1_b, w_up, conv_w, conv_b, w_down, ln2_g, ln2_b):
    bsz, seq, _ = x.shape
    t = bsz * seq
    ts = 32
    n_chunks = seq // CHUNK

    q0, k0, v0, g0 = SSM_WIDTH, SSM_WIDTH + ATTN_WIDTH, SSM_WIDTH + 2 * ATTN_WIDTH, SSM_WIDTH + 3 * ATTN_WIDTH
    col_scale = np.ones((g0,), np.float32)
    col_scale[q0:k0] = HEAD_DIM ** -0.5
    w_proj = (w_in[:, :g0] * col_scale).astype(BF16)
    b_proj = (b_in[:g0] * col_scale)[None, :]
    inv_freq = jnp.power(ROPE_THETA, -jnp.arange(32, dtype=F32) * 2.0 / HEAD_DIM)
    invf = jnp.tile(inv_freq, 4)[None, :]
    sgn = jnp.asarray(np.tile(np.repeat(np.float32([-1.0, 1.0]), 32), 2))[None, :]
    step_perm = _step_major_perm(bsz, ts)

    pos_tiles = jnp.transpose(positions.reshape(bsz, seq // ts, ts), (1, 0, 2))
    vp, q, k, v = _proj(x, pos_tiles, w_proj, b_proj, invf, sgn,
                        jnp.asarray(step_perm, dtype=BF16), ts)

    lag_pair, q_pair, p_pair, dec = _ssm_params(lam_re, lam_im, log_dt, b_re, b_im, c_re, c_im, d_skip)
    z = _ssm(vp, lag_pair, q_pair, p_pair, dec, n_chunks, bsz)

    attn = _attn(q, k, v, _interleave_perms(), _band_bias())

    wg = w_in[:, g0:].astype(BF16)
    bg = b_in[g0:][None, :]
    wglu = jnp.concatenate([w_glu_v, w_glu_g], axis=1).astype(BF16)
    h = _mix(x, z, attn, jnp.asarray(step_perm.T, dtype=BF16), wg, bg, wglu, w_attn_br.astype(BF16),
             w_out.astype(BF16), ln1_g[None, :], ln1_b[None, :], ts)

    cw = jnp.concatenate([conv_w, jnp.zeros((5, 2 * D_FF), F32)], axis=0)
    out = _ffn(h.reshape(t, D_MODEL), w_up.astype(BF16), cw, conv_b[None, :], w_down.astype(BF16),
               ln2_g[None, :], ln2_b[None, :], FFN_TM, seq)
    return out.reshape(bsz, seq, D_MODEL)


def kernel(x, positions, w_in, b_in, ssm_lam_re, ssm_lam_im, ssm_log_dt, ssm_b_re, ssm_b_im, ssm_c_re, ssm_c_im, ssm_d, w_glu_v, w_glu_g, w_attn_br, w_out, ln1_g, ln1_b, w_up, conv_w, conv_b, w_down, ln2_g, ln2_b):
    h = x
    for layer in range(w_in.shape[0]):
        h = _layer(h, positions, w_in[layer], b_in[layer], ssm_lam_re[layer], ssm_lam_im[layer],
                   ssm_log_dt[layer], ssm_b_re[layer], ssm_b_im[layer], ssm_c_re[layer], ssm_c_im[layer],
                   ssm_d[layer], w_glu_v[layer], w_glu_g[layer], w_attn_br[layer], w_out[layer],
                   ln1_g[layer], ln1_b[layer], w_up[layer], conv_w[layer], conv_b[layer], w_down[layer],
                   ln2_g[layer], ln2_b[layer])
    return h
```

```python
import functools
import math

import numpy as np
import jax
import jax.numpy as jnp
from jax import lax
from jax.experimental import pallas as pl
from jax.experimental.pallas import tpu as pltpu

D_MODEL = 1024
SSM_WIDTH = 512
SSM_GROUP = 16
SSM_GROUPS = 32
SSM_STATE = 64
HEAD_DIM = 64
HEADS_PER_GROUP = 4
ATTN_PATTERNS = ((128, 1), (512, 4), (2048, 16))
N_HEADS = 12
ATTN_WIDTH = 768
ATTN_OUT_WIDTH = 256
D_FF = 2816
LN_EPS = 1e-5
NEG_INF = -1e30
ROPE_THETA = 10000.0
DEEPNORM_ALPHA = 2.0 ** 0.25

CHUNK = 16
PAIR_W = 2 * SSM_GROUP * CHUNK
N_PAIRS = SSM_GROUPS // 2
LAG_LANES = 1024
HALF_WIN = 64
QBLK = 128
KBLK = 256
PTILE = 256
PROJ_W = SSM_WIDTH + 3 * ATTN_WIDTH
VMEM_LIMIT = 56 * 1024 * 1024
F32 = jnp.float32
BF16 = jnp.bfloat16


def _dot(a, b):
    return jnp.dot(a, b, preferred_element_type=F32)


def _gelu(x):
    return 0.5 * x * (1.0 + lax.erf(x * np.float32(math.sqrt(0.5))))


def _layer_norm(r, g, b):
    mu = jnp.mean(r, axis=-1, keepdims=True)
    c = r - mu
    var = jnp.mean(c * c, axis=-1, keepdims=True)
    return c * lax.rsqrt(var + LN_EPS) * g + b


def _params(n_axes):
    return pltpu.CompilerParams(dimension_semantics=("arbitrary",) * n_axes, vmem_limit_bytes=VMEM_LIMIT)


def _place(r, n_rows):
    lane_grp = lax.broadcasted_iota(jnp.int32, (n_rows, 128), 1) // 32
    out = [[None] * 4 for _ in range(N_PAIRS)]
    for j in range(4):
        for q in range(4):
            src = [r[(4 * j + m) * n_rows:(4 * j + m + 1) * n_rows, 128 * q:128 * (q + 1)] for m in range(4)]
            rolled = [[s if sh == 0 else pltpu.roll(s, 32 * sh, 1) for sh in range(4)] for s in src]
            for pp in range(4):
                d = rolled[0][(0 - pp) % 4]
                for m in range(1, 4):
                    d = jnp.where(lane_grp == m, rolled[m][(m - pp) % 4], d)
                out[4 * q + pp][j] = d
    return out


def _unplace(zs, n_rows):
    lane_grp = lax.broadcasted_iota(jnp.int32, (n_rows, 128), 1) // 32
    rows = []
    for j in range(4):
        for m in range(4):
            cols = []
            for q in range(4):
                d = None
                for pp in range(4):
                    s = zs[4 * q + pp][j]
                    sh = (pp - m) % 4
                    rl = s if sh == 0 else pltpu.roll(s, 32 * sh, 1)
                    d = rl if d is None else jnp.where(lane_grp == pp, rl, d)
                cols.append(d)
            rows.append(jnp.concatenate(cols, axis=1))
    return jnp.concatenate(rows, axis=0)


def _proj_kernel(x_ref, pos_ref, w_ref, b_ref, invf_ref, sgn_ref, perm_ref, vp_ref, q_ref, k_ref, v_ref, *, bsz, ts):
    tm = bsz * ts
    xb = x_ref[...].reshape(tm, D_MODEL).astype(BF16)
    posf = jnp.broadcast_to(pos_ref[...].astype(F32)[:, None, :], (bsz, ts, ts)).reshape(tm, ts)
    own = (lax.broadcasted_iota(jnp.int32, (tm, ts), 0) & (ts - 1)) == lax.broadcasted_iota(jnp.int32, (tm, ts), 1)
    pos_col = jnp.sum(jnp.where(own, posf, 0.0), axis=1, keepdims=True)
    ang = pos_col * invf_ref[...]
    cos = jnp.cos(ang)
    sin = jnp.sin(ang) * sgn_ref[...]
    low_half = (lax.broadcasted_iota(jnp.int32, (tm, 128), 1) & 32) == 0
    u = _dot(xb, w_ref[:, 0:SSM_WIDTH]) + b_ref[:, 0:SSM_WIDTH]
    r = _dot(perm_ref[...], u.astype(BF16))
    placed = _place(r, tm // CHUNK)
    for p in range(N_PAIRS):
        for j in range(4):
            vp_ref[p, :, 128 * j:128 * (j + 1)] = placed[p][j].astype(BF16)
    for dst, c0 in ((q_ref, SSM_WIDTH), (k_ref, SSM_WIDTH + ATTN_WIDTH)):
        t = _dot(xb, w_ref[:, c0:c0 + ATTN_WIDTH]) + b_ref[:, c0:c0 + ATTN_WIDTH]
        for j in range(ATTN_WIDTH // 128):
            tj = t[:, 128 * j:128 * (j + 1)]
            partner = jnp.where(low_half, pltpu.roll(tj, 96, 1), pltpu.roll(tj, 32, 1))
            dst[:, :, 128 * j:128 * (j + 1)] = (tj * cos + partner * sin).astype(BF16).reshape(bsz, ts, 128)
    c0 = SSM_WIDTH + 2 * ATTN_WIDTH
    v = _dot(xb, w_ref[:, c0:c0 + ATTN_WIDTH]) + b_ref[:, c0:c0 + ATTN_WIDTH]
    v_ref[...] = v.astype(BF16).reshape(bsz, ts, ATTN_WIDTH)


def _proj(x, pos, w, b, invf, sgn, perm, ts):
    bsz, seq, _ = x.shape
    tm = bsz * ts
    rows = tm // CHUNK
    tile = lambda w_: pl.BlockSpec((bsz, ts, w_), lambda i: (0, i, 0))
    fixed = lambda i: (0, 0)
    return pl.pallas_call(
        functools.partial(_proj_kernel, bsz=bsz, ts=ts),
        grid=(seq // ts,),
        in_specs=[tile(D_MODEL), pl.BlockSpec((None, bsz, ts), lambda i: (i, 0, 0)),
                  pl.BlockSpec((D_MODEL, PROJ_W), fixed, pipeline_mode=pl.Buffered(1)),
                  pl.BlockSpec((1, PROJ_W), fixed),
                  pl.BlockSpec((1, 128), fixed),
                  pl.BlockSpec((1, 128), fixed),
                  pl.BlockSpec((tm, tm), fixed, pipeline_mode=pl.Buffered(1))],
        out_specs=[pl.BlockSpec((N_PAIRS, rows, PAIR_W), lambda i: (0, i, 0)),
                   tile(ATTN_WIDTH), tile(ATTN_WIDTH), tile(ATTN_WIDTH)],
        out_shape=[jax.ShapeDtypeStruct((N_PAIRS, seq // CHUNK * bsz, PAIR_W), BF16),
                   jax.ShapeDtypeStruct((bsz, seq, ATTN_WIDTH), BF16),
                   jax.ShapeDtypeStruct((bsz, seq, ATTN_WIDTH), BF16),
                   jax.ShapeDtypeStruct((bsz, seq, ATTN_WIDTH), BF16)],
        compiler_params=_params(1),
        name="proj",
    )(x, pos, w, b, invf, sgn, perm)


def _ssm_kernel(v_ref, lag_ref, q_ref, p_ref, a_ref, z_ref, h_ref, m_ref, *, n_chunks, bsz):
    lags = lag_ref[...]
    for s in range(CHUNK):
        off = 32 * (CHUNK - 1 - s)
        win = lags[:, :PAIR_W] if off == 0 else pltpu.roll(lags, LAG_LANES - off, 1)[:, :PAIR_W]
        m_ref[32 * s:32 * (s + 1), :] = win.astype(BF16)
    v = v_ref[...]
    h_ref[...] = _dot(v, q_ref[...])
    dec = a_ref[...]
    afr, afi, abr, abi = (jnp.broadcast_to(dec[i:i + 1, :], (bsz, 128)) for i in range(4))

    def step(k, carry):
        hfr, hfi, hbr, hbi = carry
        rf = pl.ds(pl.multiple_of(k * bsz, bsz), bsz)
        rb = pl.ds(pl.multiple_of((n_chunks - 1 - k) * bsz, bsz), bsz)
        xfr = h_ref[rf, 0:128]
        xfi = h_ref[rf, 128:256]
        xbr = h_ref[rb, 256:384]
        xbi = h_ref[rb, 384:512]
        h_ref[rf, 0:128] = hfr
        h_ref[rf, 128:256] = hfi
        h_ref[rb, 256:384] = hbr
        h_ref[rb, 384:512] = hbi
        return (afr * hfr - afi * hfi + xfr, afr * hfi + afi * hfr + xfi,
                abr * hbr - abi * hbi + xbr, abr * hbi + abi * hbr + xbi)

    zero = jnp.zeros((bsz, 128), F32)
    lax.fori_loop(0, n_chunks, step, (zero, zero, zero, zero), unroll=4)
    y = _dot(v, m_ref[...]) + _dot(h_ref[...].astype(BF16), p_ref[...])
    z_ref[...] = _gelu(y).astype(BF16)


def _ssm(vp, lags, q, p, a, n_chunks, bsz):
    rows = n_chunks * bsz
    blk = lambda shape: pl.BlockSpec((None,) + shape, lambda i: (i, 0, 0))
    return pl.pallas_call(
        functools.partial(_ssm_kernel, n_chunks=n_chunks, bsz=bsz),
        grid=(N_PAIRS,),
        in_specs=[blk((rows, PAIR_W)), blk((2 * SSM_GROUP, LAG_LANES)), blk((PAIR_W, PAIR_W)),
                  blk((PAIR_W, PAIR_W)), blk((8, 128))],
        out_specs=blk((rows, PAIR_W)),
        out_shape=jax.ShapeDtypeStruct((N_PAIRS, rows, PAIR_W), BF16),
        scratch_shapes=[pltpu.VMEM((rows, PAIR_W), F32), pltpu.VMEM((PAIR_W, PAIR_W), BF16)],
        compiler_params=_params(1),
        name="ssm",
    )(vp, lags, q, p, a)


def _ssm_params(lam_re, lam_im, log_dt, b_re, b_im, c_re, c_im, d_skip):
    hp = lax.Precision.HIGH
    dt = jnp.exp(log_dt)[..., None]
    xr, xi = lam_re * dt, lam_im * dt
    abar_m1_r = jnp.expm1(xr) * jnp.cos(xi) - 2.0 * jnp.square(jnp.sin(0.5 * xi))
    abar_i = jnp.exp(xr) * jnp.sin(xi)
    den = lam_re * lam_re + lam_im * lam_im
    kr = (abar_m1_r * lam_re + abar_i * lam_im) / den
    ki = (abar_i * lam_re - abar_m1_r * lam_im) / den
    bbr = kr[..., None] * b_re - ki[..., None] * b_im
    bbi = kr[..., None] * b_im + ki[..., None] * b_re
    n = jnp.arange(CHUNK + 1, dtype=F32)[:, None, None, None]
    mag = jnp.exp(n * xr[None])
    pr, pi = mag * jnp.cos(n * xi[None]), mag * jnp.sin(n * xi[None])
    wr = c_re[None] * pr[:, :, :, None, :] - c_im[None] * pi[:, :, :, None, :]
    wi = c_re[None] * pi[:, :, :, None, :] + c_im[None] * pr[:, :, :, None, :]
    kern = jnp.einsum('ndgop,dgpc->ndgoc', jnp.concatenate([wr, -wi], axis=-1),
                      jnp.concatenate([bbr, bbi], axis=-2), precision=hp)
    skip = jnp.eye(SSM_GROUP, dtype=F32)[None] * d_skip.reshape(SSM_GROUPS, SSM_GROUP)[:, :, None]
    lags = jnp.concatenate([kern[CHUNK - 1:0:-1, 1], (kern[0, 0] + kern[0, 1] + skip)[None], kern[1:CHUNK, 0]], axis=0)
    lags = jnp.transpose(lags.reshape(2 * CHUNK - 1, N_PAIRS, 2, SSM_GROUP, SSM_GROUP), (1, 2, 4, 0, 3))
    lag_pair = lags[:, :, :, :, None, :] * jnp.eye(2, dtype=F32)[None, :, None, None, :, None]
    lag_pair = lag_pair.reshape(N_PAIRS, 2 * SSM_GROUP, (2 * CHUNK - 1) * 2 * SSM_GROUP)
    lag_pair = jnp.pad(lag_pair, ((0, 0), (0, 0), (0, LAG_LANES - lag_pair.shape[-1])))
    pw_f = jnp.arange(CHUNK - 1, -1, -1)
    pw_b = jnp.arange(CHUNK)

    def in_map(d, pw):
        ar, ai = pr[pw, d], pi[pw, d]
        qr = ar[..., None] * bbr[d][None] - ai[..., None] * bbi[d][None]
        qi = ar[..., None] * bbi[d][None] + ai[..., None] * bbr[d][None]
        f = lambda z: jnp.transpose(z, (1, 0, 3, 2)).reshape(SSM_GROUPS, CHUNK * SSM_GROUP, SSM_STATE)
        return f(qr), f(qi)

    qfr, qfi = in_map(0, pw_f)
    qbr, qbi = in_map(1, pw_b)
    pw_of = jnp.arange(1, CHUNK + 1)
    pw_ob = jnp.arange(CHUNK, 0, -1)

    def out_map(d, pw):
        g = lambda z: jnp.transpose(z[pw, d], (1, 3, 0, 2)).reshape(SSM_GROUPS, SSM_STATE, CHUNK * SSM_GROUP)
        return g(wr), -g(wi)

    pfr, pfi = out_map(0, pw_of)
    pbr, pbi = out_map(1, pw_ob)

    def pair_cols(parts):
        out = jnp.zeros((N_PAIRS, PAIR_W, 4, 2, SSM_STATE), BF16)
        for pi_, z in enumerate(parts):
            zz = z.astype(BF16).reshape(N_PAIRS, 2, CHUNK * SSM_GROUP, SSM_STATE)
            for gl in range(2):
                out = out.at[:, gl * 256:(gl + 1) * 256, pi_, gl, :].set(zz[:, gl])
        return out.reshape(N_PAIRS, PAIR_W, PAIR_W)

    def pair_rows(parts):
        out = jnp.zeros((N_PAIRS, 4, 2, SSM_STATE, PAIR_W), BF16)
        for pi_, z in enumerate(parts):
            zz = z.astype(BF16).reshape(N_PAIRS, 2, SSM_STATE, CHUNK * SSM_GROUP)
            for gl in range(2):
                out = out.at[:, pi_, gl, :, gl * 256:(gl + 1) * 256].set(zz[:, gl])
        return out.reshape(N_PAIRS, PAIR_W, PAIR_W)

    q_pair = pair_cols([qfr, qfi, qbr, qbi])
    p_pair = pair_rows([pfr, pfi, pbr, pbi])
    new = np.arange(PAIR_W)
    old = ((new // 16) % 2) * 256 + (new // 32) * 16 + new % 16
    sel = np.zeros((PAIR_W, PAIR_W), np.float32)
    sel[new, old] = 1.0
    sel = jnp.asarray(sel, dtype=BF16)
    rows_step_major = lambda z: jnp.einsum('ij,pjk->pik', sel, z, preferred_element_type=BF16)
    cols_step_major = lambda z: jnp.einsum('pjk,lk->pjl', z, sel, preferred_element_type=BF16)
    q_pair = rows_step_major(q_pair)
    p_pair = cols_step_major(p_pair)
    dec = jnp.stack([pr[CHUNK, 0], pi[CHUNK, 0], pr[CHUNK, 1], pi[CHUNK, 1]], axis=0)
    dec = jnp.transpose(dec.reshape(4, N_PAIRS, 128), (1, 0, 2))
    dec = jnp.concatenate([dec, jnp.zeros((N_PAIRS, 4, 128), F32)], axis=1)
    return lag_pair, q_pair, p_pair, dec


def _banded(q_ref, k_ref, v_ref, bias_ref, o_ref, m_ref, d_ref, *, seq, n_sub):
    lane = lax.broadcasted_iota(jnp.int32, (QBLK, 128), 1)
    low64 = lane < 64
    head0 = low64
    zero = jnp.zeros((QBLK, 128), BF16)
    ones = jnp.ones((KBLK, 128), BF16)
    blocks_per_seq = n_sub // QBLK

    def block(i, carry):
        base = (i // blocks_per_seq) * n_sub
        m0 = (i % blocks_per_seq) * QBLK
        ks = jnp.clip(m0 - HALF_WIN, 0, n_sub - KBLK)
        bias = bias_ref[(m0 - ks) // HALF_WIN]
        qrow = pl.multiple_of(base + m0, QBLK)
        krow = pl.multiple_of(base + ks, HALF_WIN)
        for pi_ in range(2):
            cs = slice(128 * pi_, 128 * (pi_ + 1))
            q2 = q_ref[pl.ds(qrow, QBLK), cs]
            k2 = k_ref[pl.ds(krow, KBLK), cs]
            v2 = jnp.concatenate([v_ref[pl.ds(krow, KBLK), cs], ones], axis=1)
            qq = jnp.concatenate([jnp.where(head0, q2, zero), jnp.where(head0, zero, q2)], axis=0)
            s = lax.dot_general(qq, k2, (((1,), (1,)), ((), ())), preferred_element_type=F32)
            s = s + bias
            mx = jnp.max(s, axis=-1, keepdims=True)
            pv = _dot(jnp.exp(s - mx).astype(BF16), v2)
            rows = pl.ds(qrow, QBLK)
            o_ref[rows, cs] = jnp.where(low64, pv[:QBLK, :128], pv[QBLK:, :128]).astype(o_ref.dtype)
            m_ref[rows, cs] = jnp.where(low64, mx[:QBLK], mx[QBLK:])
            d_ref[rows, cs] = jnp.where(low64, pv[:QBLK, 128:], pv[QBLK:, 128:]).astype(d_ref.dtype)
        return carry

    lax.fori_loop(0, seq // QBLK, block, 0, unroll=16)


def _attn_kernel(q_ref, k_ref, v_ref, perm_ref, bias_ref, out_ref, qd_ref, kd_ref, vd_ref, od_ref, dd_ref, md_ref,
                 num_ref, mx_ref, den_ref, *, seq):
    g = pl.program_id(1)
    n_tiles = seq // PTILE

    def deinterleave(dil, pidx):
        n_loc = PTILE // dil
        n_sub = seq // dil
        pm = perm_ref[pidx]

        def tile(j, carry):
            rows = pl.ds(pl.multiple_of(j * PTILE, PTILE), PTILE)
            for src, dst in ((q_ref, qd_ref), (k_ref, kd_ref), (v_ref, vd_ref)):
                y = _dot(pm, src[rows, :]).astype(BF16)
                for r in range(dil):
                    dst[pl.ds(pl.multiple_of(r * n_sub + j * n_loc, n_loc), n_loc), :] = y[r * n_loc:(r + 1) * n_loc]
            return carry

        lax.fori_loop(0, n_tiles, tile, 0, unroll=8)

    def merge(dil, pidx, last):
        n_loc = PTILE // dil
        n_sub = seq // dil
        pm = perm_ref[pidx]

        def tile(j, carry):
            rows = pl.ds(pl.multiple_of(j * PTILE, PTILE), PTILE)

            def gathered(ref):
                return jnp.concatenate(
                    [ref[pl.ds(pl.multiple_of(r * n_sub + j * n_loc, n_loc), n_loc), :] for r in range(dil)], axis=0)

            o = _dot(pm, gathered(od_ref))
            d_new = _dot(pm, gathered(dd_ref))
            t = gathered(md_ref)
            hi = t.astype(BF16)
            lo = (t - hi.astype(F32)).astype(BF16)
            m_new = _dot(pm, hi) + _dot(pm, lo)
            m_old = mx_ref[rows, :]
            mx = jnp.maximum(m_old, m_new)
            a, b = jnp.exp(m_old - mx), jnp.exp(m_new - mx)
            num = num_ref[rows, :] * a + o * b
            den = den_ref[rows, :] * a + d_new * b
            if last:
                out_ref[rows, :] = (num / den).astype(BF16)
            else:
                num_ref[rows, :] = num
                mx_ref[rows, :] = mx
                den_ref[rows, :] = den
            return carry

        lax.fori_loop(0, n_tiles, tile, 0, unroll=8)

    n_groups = len(ATTN_PATTERNS)
    for gi, (_, dil) in enumerate(ATTN_PATTERNS):
        @pl.when(g == gi)
        def _():
            if gi == 0:
                _banded(q_ref, k_ref, v_ref, bias_ref, num_ref, mx_ref, den_ref, seq=seq, n_sub=seq)
            else:
                deinterleave(dil, 2 * (gi - 1))
                _banded(qd_ref, kd_ref, vd_ref, bias_ref, od_ref, md_ref, dd_ref, seq=seq, n_sub=seq // dil)
                merge(dil, 2 * (gi - 1) + 1, gi == n_groups - 1)


def _band_bias():
    r = np.arange(2 * QBLK)[:, None] & (QBLK - 1)
    j = np.arange(KBLK)[None, :]
    return jnp.asarray(np.stack([np.where(np.abs(j - HALF_WIN * i - r) <= HALF_WIN, 0.0, NEG_INF)
                                 for i in range(3)]), dtype=F32)


def _attn(q, k, v, perms, bias):
    bsz, seq, _ = q.shape
    w = ATTN_OUT_WIDTH
    blk = pl.BlockSpec((None, seq, w), lambda b, g: (b, 0, g))
    return pl.pallas_call(
        functools.partial(_attn_kernel, seq=seq),
        grid=(bsz, len(ATTN_PATTERNS)),
        in_specs=[blk, blk, blk,
                  pl.BlockSpec(perms.shape, lambda b, g: (0, 0, 0), pipeline_mode=pl.Buffered(1)),
                  pl.BlockSpec(bias.shape, lambda b, g: (0, 0, 0), pipeline_mode=pl.Buffered(1))],
        out_specs=pl.BlockSpec((None, seq, w), lambda b, g: (b, 0, 0)),
        out_shape=jax.ShapeDtypeStruct((bsz, seq, w), BF16),
        scratch_shapes=[pltpu.VMEM((seq, w), BF16)] * 5 + [pltpu.VMEM((seq, w), F32)] * 4,
        compiler_params=_params(2),
        name="attn",
    )(q, k, v, perms, bias)


def _interleave_perms():
    mats = []
    for _, dil in ATTN_PATTERNS[1:]:
        n_loc = PTILE // dil
        p = np.zeros((PTILE, PTILE), np.float32)
        pos = np.arange(PTILE)
        p[(pos % dil) * n_loc + pos // dil, pos] = 1.0
        mats.extend([p, p.T])
    return jnp.asarray(np.stack(mats), dtype=BF16)


def _mix_kernel(x_ref, z_ref, a_ref, perm_ref, wg_ref, bg_ref, wglu_ref, wbr_ref, wout_ref, g_ref, b_ref, h_ref,
                *, bsz, ts):
    tm = bsz * ts
    x = x_ref[...].reshape(tm, D_MODEL)
    xb = x.astype(BF16)
    gates = jax.nn.sigmoid(_dot(xb, wg_ref[...]) + bg_ref[...])
    zs = [[z_ref[p, :, 128 * j:128 * (j + 1)].astype(F32) for j in range(4)] for p in range(N_PAIRS)]
    z = _dot(perm_ref[...], _unplace(zs, tm // CHUNK).astype(BF16)).astype(BF16)
    gv = _dot(z, wglu_ref[...])
    ssm_out = gv[:, :D_MODEL] * jax.nn.sigmoid(gv[:, D_MODEL:])
    attn_out = _dot(a_ref[...].reshape(tm, ATTN_OUT_WIDTH), wbr_ref[...])
    mix = gates[:, :D_MODEL] * ssm_out + gates[:, D_MODEL:] * attn_out
    mixb = mix.astype(BF16)
    hb = bsz // 2
    for b0 in (0, hb):
        rs = slice(b0 * ts, (b0 + hb) * ts)
        mixed = _dot(mixb[rs], wout_ref[...])
        h = _layer_norm(DEEPNORM_ALPHA * x[rs] + mixed, g_ref[...], b_ref[...])
        h_ref[b0:b0 + hb] = h.reshape(hb, ts, D_MODEL)


def _mix(x, z, attn, perm_t, wg, bg, wglu, wbr, wout, g, b, ts):
    bsz, seq, _ = x.shape
    tm = bsz * ts
    tile = lambda w_: pl.BlockSpec((bsz, ts, w_), lambda i: (0, i, 0))
    wspec = lambda shape: pl.BlockSpec(shape, lambda i: (0, 0), pipeline_mode=pl.Buffered(1))
    return pl.pallas_call(
        functools.partial(_mix_kernel, bsz=bsz, ts=ts),
        grid=(seq // ts,),
        in_specs=[tile(D_MODEL), pl.BlockSpec((N_PAIRS, tm // CHUNK, PAIR_W), lambda i: (0, i, 0)),
                  tile(ATTN_OUT_WIDTH), wspec((tm, tm)),
                  wspec((D_MODEL, 2 * D_MODEL)), wspec((1, 2 * D_MODEL)), wspec((SSM_WIDTH, 2 * D_MODEL)),
                  wspec((ATTN_OUT_WIDTH, D_MODEL)), wspec((D_MODEL, D_MODEL)), wspec((1, D_MODEL)), wspec((1, D_MODEL))],
        out_specs=tile(D_MODEL),
        out_shape=jax.ShapeDtypeStruct((bsz, seq, D_MODEL), F32),
        compiler_params=_params(1),
        name="mix",
    )(x, z, attn, perm_t, wg, bg, wglu, wbr, wout, g, b)


FFN_CHUNK = 256
FFN_TM = 1024
FFN_HALO = 16


def _ffn_kernel(h_ref, hp_ref, hn_ref, wup_ref, cw_ref, cb_ref, wdn_ref, g_ref, b_ref, o_ref, act_ref,
                *, tm, tiles_per_seq):
    i = pl.program_id(0)
    j = i % tiles_per_seq
    h = h_ref[...]
    hb = h.astype(BF16)
    prev_ok = (j > 0).astype(F32)
    next_ok = (j < tiles_per_seq - 1).astype(F32)
    hpb = (hp_ref[...] * prev_ok).astype(BF16)
    hnb = (hn_ref[...] * next_ok).astype(BF16)
    hext = jnp.concatenate([hpb, hb, hnb], axis=0)
    rows = tm + 2 * FFN_HALO
    mid = slice(FFN_HALO, FFN_HALO + tm)
    for c in range(D_FF // FFN_CHUNK):
        halves = []
        for c0 in (c * FFN_CHUNK, D_FF + c * FFN_CHUNK):
            u = _dot(hext, wup_ref[:, c0:c0 + FFN_CHUNK])
            cw = cw_ref[:, c0:c0 + FFN_CHUNK]
            before = pltpu.roll(u, 1, 0)[mid]
            after = pltpu.roll(u, rows - 1, 0)[mid]
            halves.append(before * cw[0:1] + u[mid] * cw[1:2] + after * cw[2:3] + cb_ref[:, c0:c0 + FFN_CHUNK])
        a, val = halves
        act_ref[:, c * FFN_CHUNK:(c + 1) * FFN_CHUNK] = (_gelu(a) * val).astype(BF16)
    ffn = _dot(act_ref[...], wdn_ref[...])
    o_ref[...] = _layer_norm(DEEPNORM_ALPHA * h + ffn, g_ref[...], b_ref[...])


def _ffn(h, wup, cw, cb, wdn, g, b, tm, seq):
    t = h.shape[0]
    tps = seq // tm
    nblk = t // FFN_HALO
    fixed = lambda i: (0, 0)
    wspec = lambda shape: pl.BlockSpec(shape, fixed, pipeline_mode=pl.Buffered(1))
    return pl.pallas_call(
        functools.partial(_ffn_kernel, tm=tm, tiles_per_seq=tps),
        grid=(t // tm,),
        in_specs=[pl.BlockSpec((tm, D_MODEL), lambda i: (i, 0)),
                  pl.BlockSpec((FFN_HALO, D_MODEL), lambda i: (jnp.maximum(i * (tm // FFN_HALO) - 1, 0), 0)),
                  pl.BlockSpec((FFN_HALO, D_MODEL), lambda i: (jnp.minimum((i + 1) * (tm // FFN_HALO), nblk - 1), 0)),
                  wspec((D_MODEL, 2 * D_FF)), wspec((8, 2 * D_FF)), wspec((1, 2 * D_FF)),
                  wspec((D_FF, D_MODEL)), wspec((1, D_MODEL)), wspec((1, D_MODEL))],
        out_specs=pl.BlockSpec((tm, D_MODEL), lambda i: (i, 0)),
        out_shape=jax.ShapeDtypeStruct((t, D_MODEL), F32),
        scratch_shapes=[pltpu.VMEM((tm, D_FF), BF16)],
        compiler_params=_params(1),
        name="ffn",
    )(h, h, h, wup, cw, cb, wdn, g, b)


def _step_major_perm(bsz, ts):
    p = np.zeros((bsz * ts, bsz * ts), np.float32)
    b, k, s = np.meshgrid(np.arange(bsz), np.arange(ts // CHUNK), np.arange(CHUNK), indexing="ij")
    p[(s * (ts // CHUNK) + k) * bsz + b, (b * (ts // CHUNK) + k) * CHUNK + s] = 1.0
    return p


def _layer(x, positions, w_in, b_in, lam_re, lam_im, log_dt, b_re, b_im, c_re, c_im, d_skip,
           w_glu_v, w_glu_g, w_attn_br, w_out, ln1_g, ln1_b, w_up, conv_w, conv_b, w_down, ln2_g, ln2_b):
    bsz, seq, _ = x.shape
    t = bsz * seq
    ts = 32
    n_chunks = seq // CHUNK

    q0, k0, v0, g0 = SSM_WIDTH, SSM_WIDTH + ATTN_WIDTH, SSM_WIDTH + 2 * ATTN_WIDTH, SSM_WIDTH + 3 * ATTN_WIDTH
    col_scale = np.ones((g0,), np.float32)
    col_scale[q0:k0] = HEAD_DIM ** -0.5
    w_proj = (w_in[:, :g0] * col_scale).astype(BF16)
    b_proj = (b_in[:g0] * col_scale)[None, :]
    inv_freq = jnp.power(ROPE_THETA, -jnp.arange(32, dtype=F32) * 2.0 / HEAD_DIM)
    invf = jnp.tile(inv_freq, 4)[None, :]
    sgn = jnp.asarray(np.tile(np.repeat(np.float32([-1.0, 1.0]), 32), 2))[None, :]
    step_perm = _step_major_perm(bsz, ts)

    pos_tiles = jnp.transpose(positions.reshape(bsz, seq // ts, ts), (1, 0, 2))
    vp, q, k, v = _proj(x, pos_tiles, w_proj, b_proj, invf, sgn,
                        jnp.asarray(step_perm, dtype=BF16), ts)

    lag_pair, q_pair, p_pair, dec = _ssm_params(lam_re, lam_im, log_dt, b_re, b_im, c_re, c_im, d_skip)
    z = _ssm(vp, lag_pair, q_pair, p_pair, dec, n_chunks, bsz)

    attn = _attn(q, k, v, _interleave_perms(), _band_bias())

    wg = w_in[:, g0:].astype(BF16)
    bg = b_in[g0:][None, :]
    wglu = jnp.concatenate([w_glu_v, w_glu_g], axis=1).astype(BF16)
    h = _mix(x, z, attn, jnp.asarray(step_perm.T, dtype=BF16), wg, bg, wglu, w_attn_br.astype(BF16),
             w_out.astype(BF16), ln1_g[None, :], ln1_b[None, :], ts)

    cw = jnp.concatenate([conv_w, jnp.zeros((5, 2 * D_FF), F32)], axis=0)
    out = _ffn(h.reshape(t, D_MODEL), w_up.astype(BF16), cw, conv_b[None, :], w_down.astype(BF16),
               ln2_g[None, :], ln2_b[None, :], FFN_TM, seq)
    return out.reshape(bsz, seq, D_MODEL)


def kernel(x, positions, w_in, b_in, ssm_lam_re, ssm_lam_im, ssm_log_dt, ssm_b_re, ssm_b_im, ssm_c_re, ssm_c_im, ssm_d, w_glu_v, w_glu_g, w_attn_br, w_out, ln1_g, ln1_b, w_up, conv_w, conv_b, w_down, ln2_g, ln2_b):
    h = x
    for layer in range(w_in.shape[0]):
        h = _layer(h, positions, w_in[layer], b_in[layer], ssm_lam_re[layer], ssm_lam_im[layer],
                   ssm_log_dt[layer], ssm_b_re[layer], ssm_b_im[layer], ssm_c_re[layer], ssm_c_im[layer],
                   ssm_d[layer], w_glu_v[layer], w_glu_g[layer], w_attn_br[layer], w_out[layer],
                   ln1_g[layer], ln1_b[layer], w_up[layer], conv_w[layer], conv_b[layer], w_down[layer],
                   ln2_g[layer], ln2_b[layer])
    return h
```

```python
import functools
import math

import numpy as np
import jax
import jax.numpy as jnp
from jax import lax
from jax.experimental import pallas as pl
from jax.experimental.pallas import tpu as pltpu

D_MODEL = 1024
SSM_WIDTH = 512
SSM_GROUP = 16
SSM_GROUPS = 32
SSM_STATE = 64
HEAD_DIM = 64
HEADS_PER_GROUP = 4
ATTN_PATTERNS = ((128, 1), (512, 4), (2048, 16))
N_HEADS = 12
ATTN_WIDTH = 768
ATTN_OUT_WIDTH = 256
D_FF = 2816
LN_EPS = 1e-5
NEG_INF = -1e30
ROPE_THETA = 10000.0
DEEPNORM_ALPHA = 2.0 ** 0.25

CHUNK = 16
PAIR_W = 2 * SSM_GROUP * CHUNK
N_PAIRS = SSM_GROUPS // 2
LAG_LANES = 1024
HALF_WIN = 64
QBLK = 128
KBLK = 256
PTILE = 256
PROJ_W = SSM_WIDTH + 3 * ATTN_WIDTH
VMEM_LIMIT = 56 * 1024 * 1024
F32 = jnp.float32
BF16 = jnp.bfloat16


def _dot(a, b):
    return jnp.dot(a, b, preferred_element_type=F32)


def _gelu(x):
    return 0.5 * x * (1.0 + lax.erf(x * np.float32(math.sqrt(0.5))))


def _layer_norm(r, g, b):
    mu = jnp.mean(r, axis=-1, keepdims=True)
    c = r - mu
    var = jnp.mean(c * c, axis=-1, keepdims=True)
    return c * lax.rsqrt(var + LN_EPS) * g + b


def _params(n_axes):
    return pltpu.CompilerParams(dimension_semantics=("arbitrary",) * n_axes, vmem_limit_bytes=VMEM_LIMIT)


def _place(r, n_rows):
    lane_grp = lax.broadcasted_iota(jnp.int32, (n_rows, 128), 1) // 32
    out = [[None] * 4 for _ in range(N_PAIRS)]
    for j in range(4):
        for q in range(4):
            src = [r[(4 * j + m) * n_rows:(4 * j + m + 1) * n_rows, 128 * q:128 * (q + 1)] for m in range(4)]
            rolled = [[s if sh == 0 else pltpu.roll(s, 32 * sh, 1) for sh in range(4)] for s in src]
            for pp in range(4):
                d = rolled[0][(0 - pp) % 4]
                for m in range(1, 4):
                    d = jnp.where(lane_grp == m, rolled[m][(m - pp) % 4], d)
                out[4 * q + pp][j] = d
    return out


def _unplace(zs, n_rows):
    lane_grp = lax.broadcasted_iota(jnp.int32, (n_rows, 128), 1) // 32
    rows = []
    for j in range(4):
        for m in range(4):
            cols = []
            for q in range(4):
                d = None
                for pp in range(4):
                    s = zs[4 * q + pp][j]
                    sh = (pp - m) % 4
                    rl = s if sh == 0 else pltpu.roll(s, 32 * sh, 1)
                    d = rl if d is None else jnp.where(lane_grp == pp, rl, d)
                cols.append(d)
            rows.append(jnp.concatenate(cols, axis=1))
    return jnp.concatenate(rows, axis=0)


def _proj_kernel(x_ref, pos_ref, w_ref, b_ref, invf_ref, sgn_ref, perm_ref, vp_ref, q_ref, k_ref, v_ref, *, bsz, ts):
    tm = bsz * ts
    xb = x_ref[...].reshape(tm, D_MODEL).astype(BF16)
    posf = jnp.broadcast_to(pos_ref[...].astype(F32)[:, None, :], (bsz, ts, ts)).reshape(tm, ts)
    own = (lax.broadcasted_iota(jnp.int32, (tm, ts), 0) & (ts - 1)) == lax.broadcasted_iota(jnp.int32, (tm, ts), 1)
    pos_col = jnp.sum(jnp.where(own, posf, 0.0), axis=1, keepdims=True)
    q4 = tm // 4
    grp = lax.broadcasted_iota(jnp.int32, (q4, 128), 1) // 32
    pos4 = jnp.broadcast_to(pos_col[0:q4], (q4, 128))
    for g in range(1, 4):
        pos4 = jnp.where(grp == g, pos_col[g * q4:(g + 1) * q4], pos4)
    ang4 = pos4 * invf_ref[...]

    def spread(t4):
        blocks = []
        for g in range(4):
            t = t4 if g == 0 else pltpu.roll(t4, 128 - 32 * g, 1)
            t = jnp.where(grp == 0, t, pltpu.roll(t, 32, 1))
            blocks.append(jnp.where(grp < 2, t, pltpu.roll(t, 64, 1)))
        return jnp.concatenate(blocks, axis=0)

    cos = spread(jnp.cos(ang4))
    sin = spread(jnp.sin(ang4)) * sgn_ref[...]
    low_half = (lax.broadcasted_iota(jnp.int32, (tm, 128), 1) & 32) == 0
    u = _dot(xb, w_ref[:, 0:SSM_WIDTH]) + b_ref[:, 0:SSM_WIDTH]
    r = _dot(perm_ref[...], u.astype(BF16))
    placed = _place(r, tm // CHUNK)
    for p in range(N_PAIRS):
        for j in range(4):
            vp_ref[p, :, 128 * j:128 * (j + 1)] = placed[p][j].astype(BF16)
    for dst, c0 in ((q_ref, SSM_WIDTH), (k_ref, SSM_WIDTH + ATTN_WIDTH)):
        t = _dot(xb, w_ref[:, c0:c0 + ATTN_WIDTH]) + b_ref[:, c0:c0 + ATTN_WIDTH]
        for j in range(ATTN_WIDTH // 128):
            tj = t[:, 128 * j:128 * (j + 1)]
            partner = jnp.where(low_half, pltpu.roll(tj, 96, 1), pltpu.roll(tj, 32, 1))
            dst[:, :, 128 * j:128 * (j + 1)] = (tj * cos + partner * sin).astype(BF16).reshape(bsz, ts, 128)
    c0 = SSM_WIDTH + 2 * ATTN_WIDTH
    v = _dot(xb, w_ref[:, c0:c0 + ATTN_WIDTH]) + b_ref[:, c0:c0 + ATTN_WIDTH]
    v_ref[...] = v.astype(BF16).reshape(bsz, ts, ATTN_WIDTH)


def _proj(x, pos, w, b, invf, sgn, perm, ts):
    bsz, seq, _ = x.shape
    tm = bsz * ts
    rows = tm // CHUNK
    tile = lambda w_: pl.BlockSpec((bsz, ts, w_), lambda i: (0, i, 0))
    fixed = lambda i: (0, 0)
    return pl.pallas_call(
        functools.partial(_proj_kernel, bsz=bsz, ts=ts),
        grid=(seq // ts,),
        in_specs=[tile(D_MODEL), pl.BlockSpec((None, bsz, ts), lambda i: (i, 0, 0)),
                  pl.BlockSpec((D_MODEL, PROJ_W), fixed, pipeline_mode=pl.Buffered(1)),
                  pl.BlockSpec((1, PROJ_W), fixed),
                  pl.BlockSpec((1, 128), fixed),
                  pl.BlockSpec((1, 128), fixed),
                  pl.BlockSpec((tm, tm), fixed, pipeline_mode=pl.Buffered(1))],
        out_specs=[pl.BlockSpec((N_PAIRS, rows, PAIR_W), lambda i: (0, i, 0)),
                   tile(ATTN_WIDTH), tile(ATTN_WIDTH), tile(ATTN_WIDTH)],
        out_shape=[jax.ShapeDtypeStruct((N_PAIRS, seq // CHUNK * bsz, PAIR_W), BF16),
                   jax.ShapeDtypeStruct((bsz, seq, ATTN_WIDTH), BF16),
                   jax.ShapeDtypeStruct((bsz, seq, ATTN_WIDTH), BF16),
                   jax.ShapeDtypeStruct((bsz, seq, ATTN_WIDTH), BF16)],
        compiler_params=_params(1),
        name="proj",
    )(x, pos, w, b, invf, sgn, perm)


def _ssm_kernel(v_ref, lag_ref, q_ref, p_ref, a_ref, z_ref, h_ref, m_ref, *, n_chunks, bsz):
    lags = lag_ref[...]
    for s in range(CHUNK):
        off = 32 * (CHUNK - 1 - s)
        win = lags[:, :PAIR_W] if off == 0 else pltpu.roll(lags, LAG_LANES - off, 1)[:, :PAIR_W]
        m_ref[32 * s:32 * (s + 1), :] = win.astype(BF16)
    v = v_ref[...]
    h_ref[...] = _dot(v, q_ref[...])
    dec = a_ref[...]
    afr, afi, abr, abi = (jnp.broadcast_to(dec[i:i + 1, :], (bsz, 128)) for i in range(4))

    def step(k, carry):
        hfr, hfi, hbr, hbi = carry
        rf = pl.ds(pl.multiple_of(k * bsz, bsz), bsz)
        rb = pl.ds(pl.multiple_of((n_chunks - 1 - k) * bsz, bsz), bsz)
        xfr = h_ref[rf, 0:128]
        xfi = h_ref[rf, 128:256]
        xbr = h_ref[rb, 256:384]
        xbi = h_ref[rb, 384:512]
        h_ref[rf, 0:128] = hfr
        h_ref[rf, 128:256] = hfi
        h_ref[rb, 256:384] = hbr
        h_ref[rb, 384:512] = hbi
        return (afr * hfr - afi * hfi + xfr, afr * hfi + afi * hfr + xfi,
                abr * hbr - abi * hbi + xbr, abr * hbi + abi * hbr + xbi)

    zero = jnp.zeros((bsz, 128), F32)
    lax.fori_loop(0, n_chunks, step, (zero, zero, zero, zero), unroll=4)
    y = _dot(v, m_ref[...]) + _dot(h_ref[...].astype(BF16), p_ref[...])
    z_ref[...] = _gelu(y).astype(BF16)


def _ssm(vp, lags, q, p, a, n_chunks, bsz):
    rows = n_chunks * bsz
    blk = lambda shape: pl.BlockSpec((None,) + shape, lambda i: (i, 0, 0))
    return pl.pallas_call(
        functools.partial(_ssm_kernel, n_chunks=n_chunks, bsz=bsz),
        grid=(N_PAIRS,),
        in_specs=[blk((rows, PAIR_W)), blk((2 * SSM_GROUP, LAG_LANES)), blk((PAIR_W, PAIR_W)),
                  blk((PAIR_W, PAIR_W)), blk((8, 128))],
        out_specs=blk((rows, PAIR_W)),
        out_shape=jax.ShapeDtypeStruct((N_PAIRS, rows, PAIR_W), BF16),
        scratch_shapes=[pltpu.VMEM((rows, PAIR_W), F32), pltpu.VMEM((PAIR_W, PAIR_W), BF16)],
        compiler_params=_params(1),
        name="ssm",
    )(vp, lags, q, p, a)


def _ssm_params(lam_re, lam_im, log_dt, b_re, b_im, c_re, c_im, d_skip):
    hp = lax.Precision.HIGH
    dt = jnp.exp(log_dt)[..., None]
    xr, xi = lam_re * dt, lam_im * dt
    abar_m1_r = jnp.expm1(xr) * jnp.cos(xi) - 2.0 * jnp.square(jnp.sin(0.5 * xi))
    abar_i = jnp.exp(xr) * jnp.sin(xi)
    den = lam_re * lam_re + lam_im * lam_im
    kr = (abar_m1_r * lam_re + abar_i * lam_im) / den
    ki = (abar_i * lam_re - abar_m1_r * lam_im) / den
    bbr = kr[..., None] * b_re - ki[..., None] * b_im
    bbi = kr[..., None] * b_im + ki[..., None] * b_re
    n = jnp.arange(CHUNK + 1, dtype=F32)[:, None, None, None]
    mag = jnp.exp(n * xr[None])
    pr, pi = mag * jnp.cos(n * xi[None]), mag * jnp.sin(n * xi[None])
    wr = c_re[None] * pr[:, :, :, None, :] - c_im[None] * pi[:, :, :, None, :]
    wi = c_re[None] * pi[:, :, :, None, :] + c_im[None] * pr[:, :, :, None, :]
    kern = jnp.einsum('ndgop,dgpc->ndgoc', jnp.concatenate([wr, -wi], axis=-1),
                      jnp.concatenate([bbr, bbi], axis=-2), precision=hp)
    skip = jnp.eye(SSM_GROUP, dtype=F32)[None] * d_skip.reshape(SSM_GROUPS, SSM_GROUP)[:, :, None]
    lags = jnp.concatenate([kern[CHUNK - 1:0:-1, 1], (kern[0, 0] + kern[0, 1] + skip)[None], kern[1:CHUNK, 0]], axis=0)
    lags = jnp.transpose(lags.reshape(2 * CHUNK - 1, N_PAIRS, 2, SSM_GROUP, SSM_GROUP), (1, 2, 4, 0, 3))
    lag_pair = lags[:, :, :, :, None, :] * jnp.eye(2, dtype=F32)[None, :, None, None, :, None]
    lag_pair = lag_pair.reshape(N_PAIRS, 2 * SSM_GROUP, (2 * CHUNK - 1) * 2 * SSM_GROUP)
    lag_pair = jnp.pad(lag_pair, ((0, 0), (0, 0), (0, LAG_LANES - lag_pair.shape[-1])))
    pw_f = jnp.arange(CHUNK - 1, -1, -1)
    pw_b = jnp.arange(CHUNK)

    def in_map(d, pw):
        ar, ai = pr[pw, d], pi[pw, d]
        qr = ar[..., None] * bbr[d][None] - ai[..., None] * bbi[d][None]
        qi = ar[..., None] * bbi[d][None] + ai[..., None] * bbr[d][None]
        f = lambda z: jnp.transpose(z, (1, 0, 3, 2)).reshape(SSM_GROUPS, CHUNK * SSM_GROUP, SSM_STATE)
        return f(qr), f(qi)

    qfr, qfi = in_map(0, pw_f)
    qbr, qbi = in_map(1, pw_b)
    pw_of = jnp.arange(1, CHUNK + 1)
    pw_ob = jnp.arange(CHUNK, 0, -1)

    def out_map(d, pw):
        g = lambda z: jnp.transpose(z[pw, d], (1, 3, 0, 2)).reshape(SSM_GROUPS, SSM_STATE, CHUNK * SSM_GROUP)
        return g(wr), -g(wi)

    pfr, pfi = out_map(0, pw_of)
    pbr, pbi = out_map(1, pw_ob)

    def pair_cols(parts):
        out = jnp.zeros((N_PAIRS, PAIR_W, 4, 2, SSM_STATE), BF16)
        for pi_, z in enumerate(parts):
            zz = z.astype(BF16).reshape(N_PAIRS, 2, CHUNK * SSM_GROUP, SSM_STATE)
            for gl in range(2):
                out = out.at[:, gl * 256:(gl + 1) * 256, pi_, gl, :].set(zz[:, gl])
        return out.reshape(N_PAIRS, PAIR_W, PAIR_W)

    def pair_rows(parts):
        out = jnp.zeros((N_PAIRS, 4, 2, SSM_STATE, PAIR_W), BF16)
        for pi_, z in enumerate(parts):
            zz = z.astype(BF16).reshape(N_PAIRS, 2, SSM_STATE, CHUNK * SSM_GROUP)
            for gl in range(2):
                out = out.at[:, pi_, gl, :, gl * 256:(gl + 1) * 256].set(zz[:, gl])
        return out.reshape(N_PAIRS, PAIR_W, PAIR_W)

    q_pair = pair_cols([qfr, qfi, qbr, qbi])
    p_pair = pair_rows([pfr, pfi, pbr, pbi])
    new = np.arange(PAIR_W)
    old = ((new // 16) % 2) * 256 + (new // 32) * 16 + new % 16
    sel = np.zeros((PAIR_W, PAIR_W), np.float32)
    sel[new, old] = 1.0
    sel = jnp.asarray(sel, dtype=BF16)
    rows_step_major = lambda z: jnp.einsum('ij,pjk->pik', sel, z, preferred_element_type=BF16)
    cols_step_major = lambda z: jnp.einsum('pjk,lk->pjl', z, sel, preferred_element_type=BF16)
    q_pair = rows_step_major(q_pair)
    p_pair = cols_step_major(p_pair)
    dec = jnp.stack([pr[CHUNK, 0], pi[CHUNK, 0], pr[CHUNK, 1], pi[CHUNK, 1]], axis=0)
    dec = jnp.transpose(dec.reshape(4, N_PAIRS, 128), (1, 0, 2))
    dec = jnp.concatenate([dec, jnp.zeros((N_PAIRS, 4, 128), F32)], axis=1)
    return lag_pair, q_pair, p_pair, dec


def _banded(q_ref, k_ref, v_ref, bias_ref, o_ref, m_ref, d_ref, *, seq, n_sub):
    lane = lax.broadcasted_iota(jnp.int32, (QBLK, 128), 1)
    low64 = lane < 64
    head0 = low64
    zero = jnp.zeros((QBLK, 128), BF16)
    ones = jnp.ones((KBLK, 128), BF16)
    blocks_per_seq = n_sub // QBLK

    def block(i, carry):
        base = (i // blocks_per_seq) * n_sub
        m0 = (i % blocks_per_seq) * QBLK
        ks = jnp.clip(m0 - HALF_WIN, 0, n_sub - KBLK)
        bias = bias_ref[(m0 - ks) // HALF_WIN]
        qrow = pl.multiple_of(base + m0, QBLK)
        krow = pl.multiple_of(base + ks, HALF_WIN)
        for pi_ in range(2):
            cs = slice(128 * pi_, 128 * (pi_ + 1))
            q2 = q_ref[pl.ds(qrow, QBLK), cs]
            k2 = k_ref[pl.ds(krow, KBLK), cs]
            v2 = jnp.concatenate([v_ref[pl.ds(krow, KBLK), cs], ones], axis=1)
            qq = jnp.concatenate([jnp.where(head0, q2, zero), jnp.where(head0, zero, q2)], axis=0)
            s = lax.dot_general(qq, k2, (((1,), (1,)), ((), ())), preferred_element_type=F32)
            s = s + bias
            mx = jnp.max(s, axis=-1, keepdims=True)
            pv = _dot(jnp.exp(s - mx).astype(BF16), v2)
            rows = pl.ds(qrow, QBLK)
            o_ref[rows, cs] = jnp.where(low64, pv[:QBLK, :128], pv[QBLK:, :128]).astype(o_ref.dtype)
            m_ref[rows, cs] = jnp.where(low64, mx[:QBLK], mx[QBLK:])
            d_ref[rows, cs] = jnp.where(low64, pv[:QBLK, 128:], pv[QBLK:, 128:]).astype(d_ref.dtype)
        return carry

    lax.fori_loop(0, seq // QBLK, block, 0, unroll=16)


def _attn_kernel(q_ref, k_ref, v_ref, perm_ref, bias_ref, out_ref, qd_ref, kd_ref, vd_ref, od_ref, dd_ref, md_ref,
                 num_ref, mx_ref, den_ref, *, seq):
    g = pl.program_id(1)
    n_tiles = seq // PTILE

    def deinterleave(dil, pidx):
        n_loc = PTILE // dil
        n_sub = seq // dil
        pm = perm_ref[pidx]

        def tile(j, carry):
            rows = pl.ds(pl.multiple_of(j * PTILE, PTILE), PTILE)
            for src, dst in ((q_ref, qd_ref), (k_ref, kd_ref), (v_ref, vd_ref)):
                y = _dot(pm, src[rows, :]).astype(BF16)
                for r in range(dil):
                    dst[pl.ds(pl.multiple_of(r * n_sub + j * n_loc, n_loc), n_loc), :] = y[r * n_loc:(r + 1) * n_loc]
            return carry

        lax.fori_loop(0, n_tiles, tile, 0, unroll=8)

    def merge(dil, pidx, last):
        n_loc = PTILE // dil
        n_sub = seq // dil
        pm = perm_ref[pidx]

        def tile(j, carry):
            rows = pl.ds(pl.multiple_of(j * PTILE, PTILE), PTILE)

            def gathered(ref):
                return jnp.concatenate(
                    [ref[pl.ds(pl.multiple_of(r * n_sub + j * n_loc, n_loc), n_loc), :] for r in range(dil)], axis=0)

            o = _dot(pm, gathered(od_ref))
            d_new = _dot(pm, gathered(dd_ref))
            t = gathered(md_ref)
            hi = t.astype(BF16)
            lo = (t - hi.astype(F32)).astype(BF16)
            m_new = _dot(pm, hi) + _dot(pm, lo)
            m_old = mx_ref[rows, :]
            mx = jnp.maximum(m_old, m_new)
            a, b = jnp.exp(m_old - mx), jnp.exp(m_new - mx)
            num = num_ref[rows, :] * a + o * b
            den = den_ref[rows, :] * a + d_new * b
            if last:
                out_ref[rows, :] = (num / den).astype(BF16)
            else:
                num_ref[rows, :] = num
                mx_ref[rows, :] = mx
                den_ref[rows, :] = den
            return carry

        lax.fori_loop(0, n_tiles, tile, 0, unroll=8)

    n_groups = len(ATTN_PATTERNS)
    for gi, (_, dil) in enumerate(ATTN_PATTERNS):
        @pl.when(g == gi)
        def _():
            if gi == 0:
                _banded(q_ref, k_ref, v_ref, bias_ref, num_ref, mx_ref, den_ref, seq=seq, n_sub=seq)
            else:
                deinterleave(dil, 2 * (gi - 1))
                _banded(qd_ref, kd_ref, vd_ref, bias_ref, od_ref, md_ref, dd_ref, seq=seq, n_sub=seq // dil)
                merge(dil, 2 * (gi - 1) + 1, gi == n_groups - 1)


def _band_bias():
    r = np.arange(2 * QBLK)[:, None] & (QBLK - 1)
    j = np.arange(KBLK)[None, :]
    return jnp.asarray(np.stack([np.where(np.abs(j - HALF_WIN * i - r) <= HALF_WIN, 0.0, NEG_INF)
                                 for i in range(3)]), dtype=F32)


def _attn(q, k, v, perms, bias):
    bsz, seq, _ = q.shape
    w = ATTN_OUT_WIDTH
    blk = pl.BlockSpec((None, seq, w), lambda b, g: (b, 0, g))
    return pl.pallas_call(
        functools.partial(_attn_kernel, seq=seq),
        grid=(bsz, len(ATTN_PATTERNS)),
        in_specs=[blk, blk, blk,
                  pl.BlockSpec(perms.shape, lambda b, g: (0, 0, 0), pipeline_mode=pl.Buffered(1)),
                  pl.BlockSpec(bias.shape, lambda b, g: (0, 0, 0), pipeline_mode=pl.Buffered(1))],
        out_specs=pl.BlockSpec((None, seq, w), lambda b, g: (b, 0, 0)),
        out_shape=jax.ShapeDtypeStruct((bsz, seq, w), BF16),
        scratch_shapes=[pltpu.VMEM((seq, w), BF16)] * 5 + [pltpu.VMEM((seq, w), F32)] * 4,
        compiler_params=_params(2),
        name="attn",
    )(q, k, v, perms, bias)


def _interleave_perms():
    mats = []
    for _, dil in ATTN_PATTERNS[1:]:
        n_loc = PTILE // dil
        p = np.zeros((PTILE, PTILE), np.float32)
        pos = np.arange(PTILE)
        p[(pos % dil) * n_loc + pos // dil, pos] = 1.0
        mats.extend([p, p.T])
    return jnp.asarray(np.stack(mats), dtype=BF16)


def _mix_kernel(x_ref, z_ref, a_ref, perm_ref, wg_ref, bg_ref, wglu_ref, wbr_ref, wout_ref, g_ref, b_ref, h_ref,
                *, bsz, ts):
    tm = bsz * ts
    x = x_ref[...].reshape(tm, D_MODEL)
    xb = x.astype(BF16)
    gates = jax.nn.sigmoid(_dot(xb, wg_ref[...]) + bg_ref[...])
    zs = [[z_ref[p, :, 128 * j:128 * (j + 1)].astype(F32) for j in range(4)] for p in range(N_PAIRS)]
    z = _dot(perm_ref[...], _unplace(zs, tm // CHUNK).astype(BF16)).astype(BF16)
    gv = _dot(z, wglu_ref[...])
    ssm_out = gv[:, :D_MODEL] * jax.nn.sigmoid(gv[:, D_MODEL:])
    attn_out = _dot(a_ref[...].reshape(tm, ATTN_OUT_WIDTH), wbr_ref[...])
    mix = gates[:, :D_MODEL] * ssm_out + gates[:, D_MODEL:] * attn_out
    mixb = mix.astype(BF16)
    hb = bsz // 2
    for b0 in (0, hb):
        rs = slice(b0 * ts, (b0 + hb) * ts)
        mixed = _dot(mixb[rs], wout_ref[...])
        h = _layer_norm(DEEPNORM_ALPHA * x[rs] + mixed, g_ref[...], b_ref[...])
        h_ref[b0:b0 + hb] = h.reshape(hb, ts, D_MODEL)


def _mix(x, z, attn, perm_t, wg, bg, wglu, wbr, wout, g, b, ts):
    bsz, seq, _ = x.shape
    tm = bsz * ts
    tile = lambda w_: pl.BlockSpec((bsz, ts, w_), lambda i: (0, i, 0))
    wspec = lambda shape: pl.BlockSpec(shape, lambda i: (0, 0), pipeline_mode=pl.Buffered(1))
    return pl.pallas_call(
        functools.partial(_mix_kernel, bsz=bsz, ts=ts),
        grid=(seq // ts,),
        in_specs=[tile(D_MODEL), pl.BlockSpec((N_PAIRS, tm // CHUNK, PAIR_W), lambda i: (0, i, 0)),
                  tile(ATTN_OUT_WIDTH), wspec((tm, tm)),
                  wspec((D_MODEL, 2 * D_MODEL)), wspec((1, 2 * D_MODEL)), wspec((SSM_WIDTH, 2 * D_MODEL)),
                  wspec((ATTN_OUT_WIDTH, D_MODEL)), wspec((D_MODEL, D_MODEL)), wspec((1, D_MODEL)), wspec((1, D_MODEL))],
        out_specs=tile(D_MODEL),
        out_shape=jax.ShapeDtypeStruct((bsz, seq, D_MODEL), F32),
        compiler_params=_params(1),
        name="mix",
    )(x, z, attn, perm_t, wg, bg, wglu, wbr, wout, g, b)


FFN_CHUNK = 256
FFN_TM = 1024
FFN_HALO = 16


def _ffn_kernel(h_ref, hp_ref, hn_ref, wup_ref, cw_ref, cb_ref, wdn_ref, g_ref, b_ref, o_ref, act_ref,
                *, tm, tiles_per_seq):
    i = pl.program_id(0)
    j = i % tiles_per_seq
    h = h_ref[...]
    hb = h.astype(BF16)
    prev_ok = (j > 0).astype(F32)
    next_ok = (j < tiles_per_seq - 1).astype(F32)
    hpb = (hp_ref[...] * prev_ok).astype(BF16)
    hnb = (hn_ref[...] * next_ok).astype(BF16)
    hext = jnp.concatenate([hpb, hb, hnb], axis=0)
    rows = tm + 2 * FFN_HALO
    mid = slice(FFN_HALO, FFN_HALO + tm)
    for c in range(D_FF // FFN_CHUNK):
        halves = []
        for c0 in (c * FFN_CHUNK, D_FF + c * FFN_CHUNK):
            u = _dot(hext, wup_ref[:, c0:c0 + FFN_CHUNK])
            cw = cw_ref[:, c0:c0 + FFN_CHUNK]
            before = pltpu.roll(u, 1, 0)[mid]
            after = pltpu.roll(u, rows - 1, 0)[mid]
            halves.append(before * cw[0:1] + u[mid] * cw[1:2] + after * cw[2:3] + cb_ref[:, c0:c0 + FFN_CHUNK])
        a, val = halves
        act_ref[:, c * FFN_CHUNK:(c + 1) * FFN_CHUNK] = (_gelu(a) * val).astype(BF16)
    ffn = _dot(act_ref[...], wdn_ref[...])
    o_ref[...] = _layer_norm(DEEPNORM_ALPHA * h + ffn, g_ref[...], b_ref[...])


def _ffn(h, wup, cw, cb, wdn, g, b, tm, seq):
    t = h.shape[0]
    tps = seq // tm
    nblk = t // FFN_HALO
    fixed = lambda i: (0, 0)
    wspec = lambda shape: pl.BlockSpec(shape, fixed, pipeline_mode=pl.Buffered(1))
    return pl.pallas_call(
        functools.partial(_ffn_kernel, tm=tm, tiles_per_seq=tps),
        grid=(t // tm,),
        in_specs=[pl.BlockSpec((tm, D_MODEL), lambda i: (i, 0)),
                  pl.BlockSpec((FFN_HALO, D_MODEL), lambda i: (jnp.maximum(i * (tm // FFN_HALO) - 1, 0), 0)),
                  pl.BlockSpec((FFN_HALO, D_MODEL), lambda i: (jnp.minimum((i + 1) * (tm // FFN_HALO), nblk - 1), 0)),
                  wspec((D_MODEL, 2 * D_FF)), wspec((8, 2 * D_FF)), wspec((1, 2 * D_FF)),
                  wspec((D_FF, D_MODEL)), wspec((1, D_MODEL)), wspec((1, D_MODEL))],
        out_specs=pl.BlockSpec((tm, D_MODEL), lambda i: (i, 0)),
        out_shape=jax.ShapeDtypeStruct((t, D_MODEL), F32),
        scratch_shapes=[pltpu.VMEM((tm, D_FF), BF16)],
        compiler_params=_params(1),
        name="ffn",
    )(h, h, h, wup, cw, cb, wdn, g, b)


def _step_major_perm(bsz, ts):
    p = np.zeros((bsz * ts, bsz * ts), np.float32)
    b, k, s = np.meshgrid(np.arange(bsz), np.arange(ts // CHUNK), np.arange(CHUNK), indexing="ij")
    p[(s * (ts // CHUNK) + k) * bsz + b, (b * (ts // CHUNK) + k) * CHUNK + s] = 1.0
    return p


def _layer(x, positions, w_in, b_in, lam_re, lam_im, log_dt, b_re, b_im, c_re, c_im, d_skip,
           w_glu_v, w_glu_g, w_attn_br, w_out, ln1_g, ln1_b, w_up, conv_w, conv_b, w_down, ln2_g, ln2_b):
    bsz, seq, _ = x.shape
    t = bsz * seq
    ts = 32
    n_chunks = seq // CHUNK

    q0, k0, v0, g0 = SSM_WIDTH, SSM_WIDTH + ATTN_WIDTH, SSM_WIDTH + 2 * ATTN_WIDTH, SSM_WIDTH + 3 * ATTN_WIDTH
    col_scale = np.ones((g0,), np.float32)
    col_scale[q0:k0] = HEAD_DIM ** -0.5
    w_proj = (w_in[:, :g0] * col_scale).astype(BF16)
    b_proj = (b_in[:g0] * col_scale)[None, :]
    inv_freq = jnp.power(ROPE_THETA, -jnp.arange(32, dtype=F32) * 2.0 / HEAD_DIM)
    invf = jnp.tile(inv_freq, 4)[None, :]
    sgn = jnp.asarray(np.tile(np.repeat(np.float32([-1.0, 1.0]), 32), 2))[None, :]
    step_perm = _step_major_perm(bsz, ts)

    pos_tiles = jnp.transpose(positions.reshape(bsz, seq // ts, ts), (1, 0, 2))
    vp, q, k, v = _proj(x, pos_tiles, w_proj, b_proj, invf, sgn,
                        jnp.asarray(step_perm, dtype=BF16), ts)

    lag_pair, q_pair, p_pair, dec = _ssm_params(lam_re, lam_im, log_dt, b_re, b_im, c_re, c_im, d_skip)
    z = _ssm(vp, lag_pair, q_pair, p_pair, dec, n_chunks, bsz)

    attn = _attn(q, k, v, _interleave_perms(), _band_bias())

    wg = w_in[:, g0:].astype(BF16)
    bg = b_in[g0:][None, :]
    wglu = jnp.concatenate([w_glu_v, w_glu_g], axis=1).astype(BF16)
    h = _mix(x, z, attn, jnp.asarray(step_perm.T, dtype=BF16), wg, bg, wglu, w_attn_br.astype(BF16),
             w_out.astype(BF16), ln1_g[None, :], ln1_b[None, :], ts)

    cw = jnp.concatenate([conv_w, jnp.zeros((5, 2 * D_FF), F32)], axis=0)
    out = _ffn(h.reshape(t, D_MODEL), w_up.astype(BF16), cw, conv_b[None, :], w_down.astype(BF16),
               ln2_g[None, :], ln2_b[None, :], FFN_TM, seq)
    return out.reshape(bsz, seq, D_MODEL)


def kernel(x, positions, w_in, b_in, ssm_lam_re, ssm_lam_im, ssm_log_dt, ssm_b_re, ssm_b_im, ssm_c_re, ssm_c_im, ssm_d, w_glu_v, w_glu_g, w_attn_br, w_out, ln1_g, ln1_b, w_up, conv_w, conv_b, w_down, ln2_g, ln2_b):
    h = x
    for layer in range(w_in.shape[0]):
        h = _layer(h, positions, w_in[layer], b_in[layer], ssm_lam_re[layer], ssm_lam_im[layer],
                   ssm_log_dt[layer], ssm_b_re[layer], ssm_b_im[layer], ssm_c_re[layer], ssm_c_im[layer],
                   ssm_d[layer], w_glu_v[layer], w_glu_g[layer], w_attn_br[layer], w_out[layer],
                   ln1_g[layer], ln1_b[layer], w_up[layer], conv_w[layer], conv_b[layer], w_down[layer],
                   ln2_g[layer], ln2_b[layer])
    return h
```
